```python
import jax, jax.numpy as jnp
from jax import lax
import numpy as np

D_MODEL = 1024
BATCH = 8
SEQ = 2048
DEPTH = 2

ROPE_THETA = 500000.0
Q_BLOCK = 128
NEG_INF = -1e30
NORM_EPS = 1e-6
POS_OFFSET_MAX = 512
HEAD_DIM = 64
PARTIAL_ROT = HEAD_DIM // 4
N_BRANCH = 4
BRANCH_W = 4 * HEAD_DIM
D_FF = 4 * D_MODEL

MLA_HEADS = 4
MLA_NOPE = 64
MLA_ROPE = 32
MLA_V = 64
MLA_Q_LORA = 256
MLA_KV_LORA = 128

NSA_HEADS = 4
NSA_CMP_LEN = 32
NSA_CMP_STRIDE = 16
NSA_SEL_LEN = 64
NSA_TOP_N = 8
NSA_WINDOW = 512
NSA_N_PATHS = 3
NSA_FORCED_SCORE = 1e4

FOX_HEADS = 4

DSA_HEADS = 4
IDX_HEADS = 8
IDX_DIM = 32
IDX_ROT = IDX_DIM // 4
DSA_TOP_K = 256
DSA_KEEP_DIV = 4

MLA_COLS = MLA_Q_LORA + MLA_KV_LORA + MLA_ROPE
NSA_COLS = NSA_HEADS * HEAD_DIM + 6 * HEAD_DIM + NSA_N_PATHS * NSA_HEADS
FOX_COLS = 3 * FOX_HEADS * HEAD_DIM + FOX_HEADS
DSA_COLS = DSA_HEADS * HEAD_DIM + 2 * HEAD_DIM + IDX_HEADS * IDX_DIM + IDX_DIM + IDX_HEADS
GATE_COLS = N_BRANCH * D_MODEL
IN_COLS = MLA_COLS + NSA_COLS + FOX_COLS + DSA_COLS + GATE_COLS

kernel_name = "hybrid_mla_nsa_fox_dsa_gated_block"


def split_cols(z, sizes):
    cuts = [int(c) for c in np.cumsum(sizes)[:-1]]
    return jnp.split(z, cuts, axis=-1)


def rmsnorm(x, g):
    xf = x.astype(jnp.float32)
    y = xf * lax.rsqrt(jnp.mean(xf * xf, axis=-1, keepdims=True) + NORM_EPS)
    return (y * g.astype(jnp.float32)).astype(x.dtype)


def rope_tables(positions, rot_dim):
    inv = ROPE_THETA ** (-jnp.arange(0, rot_dim, 2, dtype=jnp.float32) / rot_dim)
    ang = positions.astype(jnp.float32)[..., None] * inv
    return (jnp.cos(ang), jnp.sin(ang))


def apply_rope(x, rope):
    cos, sin = rope
    half = cos.shape[-1]
    if x.ndim == 4:
        cos, sin = cos[:, :, None, :], sin[:, :, None, :]
    cos, sin = cos.astype(x.dtype), sin.astype(x.dtype)
    x1, x2 = x[..., :half], x[..., half:2 * half]
    return jnp.concatenate([x1 * cos - x2 * sin, x2 * cos + x1 * sin, x[..., 2 * half:]], axis=-1)


def masked_softmax(s, mask):
    s = jnp.where(mask, s, NEG_INF)
    m = jnp.max(s, axis=-1, keepdims=True)
    p = jnp.exp(s - m) * mask
    return p / jnp.maximum(jnp.sum(p, axis=-1, keepdims=True), 1e-30)


def sweep_query_blocks(fn, seq):
    out = lax.map(fn, jnp.arange(seq // Q_BLOCK))
    out = jnp.moveaxis(out, 0, 1)
    return out.reshape((out.shape[0], seq) + out.shape[3:])


def blocked_causal_attention(q, k, v, scale, decay=None):
    S = q.shape[1]
    key_pos = jnp.arange(S)
    decay_t = None if decay is None else jnp.swapaxes(decay, 1, 2)

    def one_block(i):
        start = i * Q_BLOCK
        qb = lax.dynamic_slice_in_dim(q, start, Q_BLOCK, axis=1)
        s = jnp.einsum('bqhd,bkhd->bhqk', qb, k).astype(jnp.float32) * scale
        if decay_t is not None:
            cq = lax.dynamic_slice_in_dim(decay_t, start, Q_BLOCK, axis=2)
            s = s + (cq[..., :, None] - decay_t[..., None, :])
        q_pos = start + jnp.arange(Q_BLOCK)
        mask = key_pos[None, :] <= q_pos[:, None]
        p = jax.nn.softmax(jnp.where(mask, s, NEG_INF), axis=-1)
        return jnp.einsum('bhqk,bkhd->bqhd', p.astype(v.dtype), v)

    return sweep_query_blocks(one_block, S)


def mla_mixer(z, q_norm_g, w_uq, kv_norm_g, w_ukv, rope):
    B, S, _ = z.shape
    H = MLA_HEADS
    c_q, c_kv, k_rope = split_cols(z, [MLA_Q_LORA, MLA_KV_LORA, MLA_ROPE])
    q = (rmsnorm(c_q, q_norm_g) @ w_uq).reshape(B, S, H, MLA_NOPE + MLA_ROPE)
    q = jnp.concatenate([q[..., :MLA_NOPE], apply_rope(q[..., MLA_NOPE:], rope)], axis=-1)
    kv = (rmsnorm(c_kv, kv_norm_g) @ w_ukv).reshape(B, S, H, MLA_NOPE + MLA_V)
    k_rope = apply_rope(k_rope, rope)
    k = jnp.concatenate([kv[..., :MLA_NOPE],
                         jnp.broadcast_to(k_rope[:, :, None, :], (B, S, H, MLA_ROPE))], axis=-1)
    v = kv[..., MLA_NOPE:]
    o = blocked_causal_attention(q, k, v, (MLA_NOPE + MLA_ROPE) ** -0.5)
    return o.reshape(B, S, H * MLA_V)


def nsa_mixer(z, cmp_w, cmp_pe, rope):
    B, S, _ = z.shape
    H, D = NSA_HEADS, HEAD_DIM
    q, kc, vc, ks, vs, kw, vw, g = split_cols(z, [H * D] + [D] * 6 + [NSA_N_PATHS * H])
    q = apply_rope(q.reshape(B, S, H, D), rope)
    kc, ks, kw = apply_rope(kc, rope), apply_rope(ks, rope), apply_rope(kw, rope)
    g = jax.nn.sigmoid(g.reshape(B, S, NSA_N_PATHS, H))
    scale = D ** -0.5
    t_pos = jnp.arange(S)
    bidx = jnp.arange(B)[:, None, None]

    n_cmp = (S - NSA_CMP_LEN) // NSA_CMP_STRIDE + 1
    cmp_start = jnp.arange(n_cmp) * NSA_CMP_STRIDE
    win_idx = cmp_start[:, None] + jnp.arange(NSA_CMP_LEN)[None, :]

    def compress(t, w, pe):
        blocks = t[:, win_idx] + pe
        return blocks.reshape(B, n_cmp, NSA_CMP_LEN * D) @ w

    k_cmp = compress(kc, cmp_w[0], cmp_pe[0])
    v_cmp = compress(vc, cmp_w[1], cmp_pe[1])
    cmp_mask = (cmp_start + NSA_CMP_LEN - 1)[None, :] <= t_pos[:, None]
    s_cmp = jnp.einsum('bthd,bcd->bhtc', q, k_cmp).astype(jnp.float32) * scale
    p_cmp = masked_softmax(s_cmp, cmp_mask)
    o_cmp = jnp.einsum('bhtc,bcd->bthd', p_cmp.astype(v_cmp.dtype), v_cmp)

    n_sb = S // NSA_SEL_LEN
    sb = jnp.arange(n_sb)
    sb_start = sb * NSA_SEL_LEN
    overlap = jnp.maximum(
        jnp.minimum(cmp_start[:, None] + NSA_CMP_LEN, sb_start[None, :] + NSA_SEL_LEN)
        - jnp.maximum(cmp_start[:, None], sb_start[None, :]), 0).astype(jnp.float32) / NSA_CMP_LEN
    imp = jnp.einsum('bhtc,cj->btj', p_cmp, overlap)
    t_blk = t_pos[:, None] // NSA_SEL_LEN
    forced = (sb[None, :] == 0) | (sb[None, :] == t_blk) | (sb[None, :] == t_blk - 1)
    imp = jnp.where(forced, NSA_FORCED_SCORE, imp)
    imp = jnp.where(sb_start[None, :] <= t_pos[:, None], imp, NEG_INF)
    n_top = min(NSA_TOP_N, n_sb)
    _, sel_idx = lax.top_k(imp, n_top)
    k_blk = ks.reshape(B, n_sb, NSA_SEL_LEN, D)
    v_blk = vs.reshape(B, n_sb, NSA_SEL_LEN, D)

    def sel_block(i):
        start = i * Q_BLOCK
        qb = lax.dynamic_slice_in_dim(q, start, Q_BLOCK, axis=1)
        ib = lax.dynamic_slice_in_dim(sel_idx, start, Q_BLOCK, axis=1)
        kg = k_blk[bidx, ib].reshape(B, Q_BLOCK, n_top * NSA_SEL_LEN, D)
        vg = v_blk[bidx, ib].reshape(B, Q_BLOCK, n_top * NSA_SEL_LEN, D)
        qp = start + jnp.arange(Q_BLOCK)
        kpos = (ib[..., None] * NSA_SEL_LEN + jnp.arange(NSA_SEL_LEN)).reshape(B, Q_BLOCK, -1)
        mask = kpos <= qp[None, :, None]
        s = jnp.einsum('bqhd,bqmd->bhqm', qb, kg).astype(jnp.float32) * scale
        p = masked_softmax(s, mask[:, None])
        return jnp.einsum('bhqm,bqmd->bqhd', p.astype(vg.dtype), vg)

    o_sel = sweep_query_blocks(sel_block, S)

    kpad = jnp.pad(kw, ((0, 0), (NSA_WINDOW, 0), (0, 0)))
    vpad = jnp.pad(vw, ((0, 0), (NSA_WINDOW, 0), (0, 0)))

    def win_block(i):
        start = i * Q_BLOCK
        qb = lax.dynamic_slice_in_dim(q, start, Q_BLOCK, axis=1)
        kb = lax.dynamic_slice_in_dim(kpad, start, NSA_WINDOW + Q_BLOCK, axis=1)
        vb = lax.dynamic_slice_in_dim(vpad, start, NSA_WINDOW + Q_BLOCK, axis=1)
        kpos = start - NSA_WINDOW + jnp.arange(NSA_WINDOW + Q_BLOCK)
        qp = start + jnp.arange(Q_BLOCK)
        mask = ((kpos[None, :] <= qp[:, None]) & (kpos[None, :] > qp[:, None] - NSA_WINDOW)
                & (kpos[None, :] >= 0))
        s = jnp.einsum('bqhd,bkd->bhqk', qb, kb).astype(jnp.float32) * scale
        p = masked_softmax(s, mask)
        return jnp.einsum('bhqk,bkd->bqhd', p.astype(vb.dtype), vb)

    o_win = sweep_query_blocks(win_block, S)

    o = (g[:, :, 0, :, None] * o_cmp + g[:, :, 1, :, None] * o_sel
         + g[:, :, 2, :, None] * o_win)
    return o.reshape(B, S, H * D)


def fox_mixer(z, f_bias):
    B, S, _ = z.shape
    H, D = FOX_HEADS, HEAD_DIM
    q, k, v, f = split_cols(z, [H * D, H * D, H * D, H])
    log_f = jax.nn.log_sigmoid((f + f_bias).astype(jnp.float32))
    c = jnp.cumsum(log_f, axis=1)
    o = blocked_causal_attention(q.reshape(B, S, H, D), k.reshape(B, S, H, D),
                                 v.reshape(B, S, H, D), D ** -0.5, decay=c)
    return o.reshape(B, S, H * D)


def dsa_mixer(z, rope_head, rope_idx):
    B, S, _ = z.shape
    H, D = DSA_HEADS, HEAD_DIM
    q, k, v, qi, ki, w = split_cols(z, [H * D, D, D, IDX_HEADS * IDX_DIM, IDX_DIM, IDX_HEADS])
    q = apply_rope(q.reshape(B, S, H, D), rope_head)
    k = apply_rope(k, rope_head)
    qi = apply_rope(qi.reshape(B, S, IDX_HEADS, IDX_DIM), rope_idx)
    ki = apply_rope(ki, rope_idx)
    n_keep = min(DSA_TOP_K, S // DSA_KEEP_DIV)
    scale = D ** -0.5
    key_pos = jnp.arange(S)
    bidx = jnp.arange(B)[:, None, None]

    def one_block(i):
        start = i * Q_BLOCK
        qb = lax.dynamic_slice_in_dim(q, start, Q_BLOCK, axis=1)
        qib = lax.dynamic_slice_in_dim(qi, start, Q_BLOCK, axis=1)
        wb = lax.dynamic_slice_in_dim(w, start, Q_BLOCK, axis=1).astype(jnp.float32)
        logits = jnp.einsum('bqhd,bkd->bqhk', qib, ki).astype(jnp.float32) * IDX_DIM ** -0.5
        score = jnp.einsum('bqhk,bqh->bqk', jax.nn.relu(logits), wb) * IDX_HEADS ** -0.5
        qp = start + jnp.arange(Q_BLOCK)
        score = jnp.where(key_pos[None, None, :] <= qp[None, :, None], score, NEG_INF)
        _, idx = lax.top_k(score, n_keep)
        kg = k[bidx, idx]
        vg = v[bidx, idx]
        s = jnp.einsum('bqhd,bqkd->bhqk', qb, kg).astype(jnp.float32) * scale
        valid = idx <= qp[None, :, None]
        p = masked_softmax(s, valid[:, None])
        return jnp.einsum('bhqk,bqkd->bqhd', p.astype(vg.dtype), vg)

    o = sweep_query_blocks(one_block, S)
    return o.reshape(B, S, H * D)


def setup_inputs(seed: int = 0) -> dict:
    key = jax.random.key(seed)
    ks = jax.random.split(key, 18)

    def nrm(k, shape, fan_in):
        return jax.random.normal(k, shape, jnp.float32) * fan_in ** -0.5

    def gain(k, shape):
        return 1.0 + 0.02 * jax.random.normal(k, shape, jnp.float32)

    x = jax.random.normal(ks[0], (BATCH, SEQ, D_MODEL), jnp.float32)
    offset = jax.random.randint(ks[1], (BATCH, 1), 0, POS_OFFSET_MAX, dtype=jnp.int32)
    positions = offset + jnp.arange(SEQ, dtype=jnp.int32)[None, :]
    return {
        "x": x,
        "positions": positions,
        "norm1_g": gain(ks[2], (DEPTH, D_MODEL)),
        "w_in": nrm(ks[3], (DEPTH, D_MODEL, IN_COLS), D_MODEL),
        "mla_q_norm_g": gain(ks[4], (DEPTH, MLA_Q_LORA)),
        "mla_w_uq": nrm(ks[5], (DEPTH, MLA_Q_LORA, MLA_HEADS * (MLA_NOPE + MLA_ROPE)), MLA_Q_LORA),
        "mla_kv_norm_g": gain(ks[6], (DEPTH, MLA_KV_LORA)),
        "mla_w_ukv": nrm(ks[7], (DEPTH, MLA_KV_LORA, MLA_HEADS * (MLA_NOPE + MLA_V)), MLA_KV_LORA),
        "nsa_cmp_pe": 0.1 * jax.random.normal(ks[8], (DEPTH, 2, NSA_CMP_LEN, HEAD_DIM), jnp.float32),
        "nsa_cmp_w": nrm(ks[9], (DEPTH, 2, NSA_CMP_LEN * HEAD_DIM, HEAD_DIM), NSA_CMP_LEN * HEAD_DIM),
        "fox_f_bias": 1.0 + 0.1 * jax.random.normal(ks[10], (DEPTH, FOX_HEADS), jnp.float32),
        "w_branch": nrm(ks[11], (DEPTH, N_BRANCH, BRANCH_W, D_MODEL), BRANCH_W),
        "w_out": nrm(ks[12], (DEPTH, D_MODEL, D_MODEL), D_MODEL),
        "norm2_g": gain(ks[13], (DEPTH, D_MODEL)),
        "w_up": nrm(ks[14], (DEPTH, D_MODEL, D_FF), D_MODEL),
        "w_down": nrm(ks[15], (DEPTH, D_FF, D_MODEL), D_FF),
        "final_g": gain(ks[16], (D_MODEL,)),
    }


def reference(x, positions, norm1_g, w_in, mla_q_norm_g, mla_w_uq, mla_kv_norm_g, mla_w_ukv,
              nsa_cmp_pe, nsa_cmp_w, fox_f_bias, w_branch, w_out, norm2_g, w_up, w_down, final_g):
    B, S, _ = x.shape
    rope_mla = rope_tables(positions, MLA_ROPE)
    rope_head = rope_tables(positions, PARTIAL_ROT)
    rope_idx = rope_tables(positions, IDX_ROT)
    for l in range(DEPTH):
        h = rmsnorm(x, norm1_g[l])
        z = h @ w_in[l]
        z_mla, z_nsa, z_fox, z_dsa, z_gate = split_cols(
            z, [MLA_COLS, NSA_COLS, FOX_COLS, DSA_COLS, GATE_COLS])
        o_mla = mla_mixer(z_mla, mla_q_norm_g[l], mla_w_uq[l], mla_kv_norm_g[l], mla_w_ukv[l], rope_mla)
        o_nsa = nsa_mixer(z_nsa, nsa_cmp_w[l], nsa_cmp_pe[l], rope_head)
        o_fox = fox_mixer(z_fox, fox_f_bias[l])
        o_dsa = dsa_mixer(z_dsa, rope_head, rope_idx)
        branches = jnp.stack([o_mla, o_nsa, o_fox, o_dsa], axis=2)
        lifted = jnp.einsum('bsnc,ncd->bsnd', branches, w_branch[l])
        gates = jax.nn.sigmoid(z_gate.reshape(B, S, N_BRANCH, D_MODEL))
        mixed = jnp.sum(gates * lifted, axis=2)
        x = x + mixed @ w_out[l]
        h2 = rmsnorm(x, norm2_g[l])
        x = x + jnp.square(jax.nn.relu(h2 @ w_up[l])) @ w_down[l]
    return rmsnorm(x, final_g)
```

```python
import functools

import numpy as np
import jax
import jax.numpy as jnp
from jax import lax
from jax.experimental import pallas as pl
from jax.experimental.pallas import tpu as pltpu

F32 = jnp.float32
BF16 = jnp.bfloat16

D_MODEL = 1024
ROPE_THETA = 500000.0
NEG_INF = -1e30
NORM_EPS = 1e-6
HEAD_DIM = 64
N_HEADS = 4
D_FF = 4 * D_MODEL
MLA_NOPE, MLA_ROPE, MLA_V = 64, 32, 64
MLA_Q_LORA, MLA_KV_LORA = 256, 128
NSA_CMP_LEN, NSA_CMP_STRIDE, NSA_SEL_LEN = 32, 16, 64
NSA_TOP_N, NSA_WINDOW, NSA_FORCED_SCORE = 8, 512, 1e4
IDX_HEADS, IDX_DIM = 8, 32
DSA_TOP_K, DSA_KEEP_DIV = 256, 4
MLA_COLS, NSA_COLS, FOX_COLS, DSA_COLS = 416, 652, 772, 680
GATE_BASE = MLA_COLS + NSA_COLS + FOX_COLS + DSA_COLS

LANES = 128
TQ = 256
TK = 256
TM = 256
VMEM_LIMIT = 56 * 1024 * 1024

P_CQ, P_CKV, P_KR = 0, 256, 384
P_NQ, P_DQ = 512, 768
P_KVC, P_KVS, P_KVW, P_KVD = 1024, 1152, 1280, 1408
P_QI, P_KI = 1536, 1792
P_G, P_W = 2048, 2176
P_FQ, P_FK, P_FV, P_FF = 2304, 2560, 2816, 3072
P_TOTAL = 3200


def _inproj_layout():
    src = -np.ones(P_TOTAL, np.int64)
    scale = np.ones(P_TOTAL, np.float32)
    mla, nsa, fox, dsa = 0, MLA_COLS, MLA_COLS + NSA_COLS, MLA_COLS + NSA_COLS + FOX_COLS
    ar = np.arange
    src[P_CQ:P_CQ + 256] = mla + ar(256)
    src[P_CKV:P_CKV + 128] = mla + 256 + ar(128)
    src[P_KR + 64:P_KR + 96] = mla + 384 + ar(32)
    src[P_NQ:P_NQ + 256] = nsa + ar(256)
    scale[P_NQ:P_NQ + 256] = HEAD_DIM ** -0.5
    src[P_DQ:P_DQ + 256] = dsa + ar(256)
    scale[P_DQ:P_DQ + 256] = HEAD_DIM ** -0.5
    for base, k_off, v_off, origin in ((P_KVC, 256, 320, nsa), (P_KVS, 384, 448, nsa),
                                       (P_KVW, 512, 576, nsa), (P_KVD, 256, 320, dsa)):
        src[base:base + 64] = origin + k_off + ar(64)
        src[base + 64:base + 128] = origin + v_off + ar(64)
    src[P_QI:P_QI + 256] = dsa + 384 + ar(256)
    scale[P_QI:P_QI + 256] = (IDX_DIM * IDX_HEADS) ** -0.5
    for r in range(IDX_HEADS):
        src[P_KI + 32 * r:P_KI + 32 * r + 32] = dsa + 640 + ar(32)
    src[P_G:P_G + 12] = nsa + 640 + ar(12)
    src[P_W:P_W + 8] = dsa + 672 + ar(8)
    src[P_FQ:P_FQ + 256] = fox + ar(256)
    scale[P_FQ:P_FQ + 256] = HEAD_DIM ** -0.5
    src[P_FK:P_FK + 256] = fox + 256 + ar(256)
    src[P_FV:P_FV + 256] = fox + 512 + ar(256)
    src[P_FF:P_FF + 4] = fox + 768 + ar(4)
    return src, scale


def _pack_cols(w, src, scale):
    cols = jnp.take(w, jnp.asarray(np.maximum(src, 0)), axis=1)
    return cols * jnp.asarray(np.where(src >= 0, scale, 0.0).astype(np.float32))[None, :]


def _mla_up_layouts():
    q_src = -np.ones(512, np.int64)
    k_src = -np.ones(512, np.int64)
    v_src = np.zeros(256, np.int64)
    for h in range(N_HEADS):
        q_src[h * 128:h * 128 + 96] = h * 96 + np.arange(96)
        k_src[h * 128:h * 128 + 64] = h * 128 + np.arange(64)
        v_src[h * 64:h * 64 + 64] = h * 128 + 64 + np.arange(64)
    return q_src, k_src, v_src


def _rope_tables(positions):
    pos = positions.astype(F32).reshape(-1)[:, None]

    def cs(rot):
        inv = ROPE_THETA ** (-jnp.arange(0, rot, 2, dtype=F32) / rot)
        ang = pos * inv[None, :]
        return jnp.cos(ang), jnp.sin(ang)

    n = pos.shape[0]
    one = lambda w: jnp.ones((n, w), F32)
    zero = lambda w: jnp.zeros((n, w), F32)
    c, s = cs(MLA_ROPE)
    cos1 = jnp.concatenate([one(64), c, c, one(32)], axis=1)
    sin1 = jnp.concatenate([zero(64), -s, s, zero(32)], axis=1)
    c, s = cs(HEAD_DIM // 4)
    cos3 = jnp.concatenate([c, c, one(48)] * 2, axis=1)
    sin3 = jnp.concatenate([-s, s, zero(48)] * 2, axis=1)
    c, s = cs(IDX_DIM // 4)
    cos5 = jnp.concatenate([c, c, one(24)] * 4, axis=1)
    sin5 = jnp.concatenate([-s, s, zero(24)] * 4, axis=1)
    return jnp.concatenate([cos1, sin1, cos3, sin3, cos5, sin5], axis=1)


def _head_masks(n_heads, width):
    lane = np.arange(width)
    return jnp.asarray((lane[None, :] // (width // n_heads) == np.arange(n_heads)[:, None]), BF16)


def _rep_mats():
    rk = np.zeros((128, 256), np.float32)
    rv = np.zeros((128, 256), np.float32)
    for h in range(N_HEADS):
        rk[np.arange(64), h * 64 + np.arange(64)] = 1.0
        rv[64 + np.arange(64), h * 64 + np.arange(64)] = 1.0
    return jnp.asarray(rk, BF16), jnp.asarray(rv, BF16)


def _rms(x, g):
    return x * lax.rsqrt(jnp.mean(x * x, axis=-1, keepdims=True) + NORM_EPS) * g


def _rope_tile(x, cos, sin, is_x1, half):
    fwd = pltpu.roll(x, LANES - half, axis=1)
    bwd = pltpu.roll(x, half, axis=1)
    return x * cos + jnp.where(is_x1, fwd, bwd) * sin


def _nt_dot(a, b):
    return lax.dot_general(a, b, (((1,), (1,)), ((), ())), preferred_element_type=F32)


def _inproj_kernel(tiles_per_seq, x_ref, g1_ref, w_ref, tab_ref, gq_ref, wuq_ref, gkv_ref, wuk_ref,
                   wuv_ref, fb_ref, a_ref, b_ref, kvc_ref, c_ref, d_ref, e_ref, carry_ref):
    i = pl.program_id(0)
    x = x_ref[...]
    hb = _rms(x, g1_ref[...]).astype(BF16)
    z = jnp.dot(hb, w_ref[...], preferred_element_type=F32)

    lane = lax.broadcasted_iota(jnp.int32, (TM, LANES), 1)
    cos1, sin1 = tab_ref[:, 0:128], tab_ref[:, 128:256]
    cos3, sin3 = tab_ref[:, 256:384], tab_ref[:, 384:512]
    cos5, sin5 = tab_ref[:, 512:640], tab_ref[:, 640:768]
    x1_mla = (lane >= 64) & (lane < 80)
    x1_head = (lane & 63) < 8
    x1_idx = (lane & 31) < 4
    k_half = lane < 64
    cos4 = jnp.where(k_half, cos3, 1.0)
    sin4 = jnp.where(k_half, sin3, 0.0)

    cqn = _rms(z[:, P_CQ:P_CQ + 256], gq_ref[...]).astype(BF16)
    q_up = jnp.dot(cqn, wuq_ref[...], preferred_element_type=F32)
    ckvn = _rms(z[:, P_CKV:P_CKV + 128], gkv_ref[...]).astype(BF16)
    k_up = jnp.dot(ckvn, wuk_ref[...], preferred_element_type=F32)
    v_up = jnp.dot(ckvn, wuv_ref[...], preferred_element_type=F32)
    k_rot = _rope_tile(z[:, P_KR:P_KR + 128], cos1, sin1, x1_mla, 16)
    for h in range(N_HEADS):
        sl = slice(h * 128, (h + 1) * 128)
        a_ref[:, sl] = _rope_tile(q_up[:, sl], cos1, sin1, x1_mla, 16).astype(BF16)
        a_ref[:, 512 + h * 128:512 + (h + 1) * 128] = (k_up[:, sl] + k_rot).astype(BF16)
    a_ref[:, 1024:1280] = v_up.astype(BF16)

    for t in range(4):
        b_ref[:, t * 128:(t + 1) * 128] = _rope_tile(
            z[:, P_NQ + t * 128:P_NQ + (t + 1) * 128], cos3, sin3, x1_head, 8).astype(BF16)
    kvc_ref[...] = _rope_tile(z[:, P_KVC:P_KVC + 128], cos4, sin4, x1_head & k_half, 8).astype(BF16)
    for t, off in enumerate((P_KVS, P_KVW, P_KVD)):
        b_ref[:, 512 + t * 128:512 + (t + 1) * 128] = _rope_tile(
            z[:, off:off + 128], cos4, sin4, x1_head & k_half, 8).astype(BF16)

    for t in range(4):
        c_ref[:, t * 128:(t + 1) * 128] = _rope_tile(
            z[:, P_QI + t * 128:P_QI + (t + 1) * 128], cos5, sin5, x1_idx, 4).astype(BF16)

    d_ref[:, 0:128] = 1.0 / (1.0 + jnp.exp(-z[:, P_G:P_G + 128]))
    d_ref[:, 128:256] = z[:, P_W:P_W + 128]
    f = z[:, P_FF:P_FF + 128] + fb_ref[...]
    log_f = jnp.minimum(f, 0.0) - jnp.log(1.0 + jnp.exp(-jnp.abs(f)))

    @pl.when(i % tiles_per_seq == 0)
    def _():
        carry_ref[...] = jnp.zeros_like(carry_ref)

    row = lax.broadcasted_iota(jnp.int32, (TM, TM), 0)
    col = lax.broadcasted_iota(jnp.int32, (TM, TM), 1)
    tri = jnp.where(col <= row, 1.0, 0.0).astype(F32)
    c = jnp.dot(tri, log_f, preferred_element_type=F32, precision=lax.Precision.HIGHEST)
    c = c + carry_ref[0:1, :]
    d_ref[:, 256:384] = c
    carry_ref[...] = jnp.broadcast_to(c[TM - 1:TM, :], carry_ref.shape)

    e_ref[...] = z[:, P_FQ:P_FQ + 768].astype(BF16)


def _inproj(x2, g1, w_packed, tab, gq, wuq, gkv, wuk, wuv, fb, seq):
    t = x2.shape[0]
    full = lambda shape: pl.BlockSpec(shape, lambda i: (0, 0))
    rows = lambda w: pl.BlockSpec((TM, w), lambda i: (i, 0))
    return pl.pallas_call(
        functools.partial(_inproj_kernel, seq // TM),
        grid=(t // TM,),
        in_specs=[rows(D_MODEL), full((1, D_MODEL)), full((D_MODEL, P_TOTAL)), rows(768),
                  full((1, 256)), full((256, 512)), full((1, 128)), full((128, 512)), full((128, 256)),
                  full((1, 128))],
        out_specs=[rows(1280), rows(896), rows(128), rows(512), rows(384), rows(768)],
        out_shape=[jax.ShapeDtypeStruct((t, 1280), BF16), jax.ShapeDtypeStruct((t, 896), BF16),
                   jax.ShapeDtypeStruct((t, 128), BF16), jax.ShapeDtypeStruct((t, 512), BF16),
                   jax.ShapeDtypeStruct((t, 384), F32), jax.ShapeDtypeStruct((t, 768), BF16)],
        scratch_shapes=[pltpu.VMEM((8, 128), F32)],
        compiler_params=pltpu.CompilerParams(dimension_semantics=("arbitrary",), vmem_limit_bytes=VMEM_LIMIT),
        name="inproj",
    )(x2, g1, w_packed, tab, gq, wuq, gkv, wuk, wuv, fb)


def _compress_kernel(r_ref, pe_ref, wt_ref, wb_ref, o_ref):
    r = r_ref[0].astype(F32)
    top = jnp.dot((r + pe_ref[0:1, :]).astype(BF16), wt_ref[...], preferred_element_type=F32)
    bot = jnp.dot((r + pe_ref[1:2, :]).astype(BF16), wb_ref[...], preferred_element_type=F32)
    n = bot.shape[0]
    o_ref[0] = (top + pltpu.roll(bot, n - 1, axis=0)).astype(BF16)


def _compress(kvc3, pe2, w_top, w_bot):
    b, n, _ = kvc3.shape
    return pl.pallas_call(
        _compress_kernel,
        grid=(b,),
        in_specs=[pl.BlockSpec((1, n, 2048), lambda i: (i, 0, 0)), pl.BlockSpec((2, 2048), lambda i: (0, 0)),
                  pl.BlockSpec((2048, 128), lambda i: (0, 0)), pl.BlockSpec((2048, 128), lambda i: (0, 0))],
        out_specs=pl.BlockSpec((1, n, 128), lambda i: (i, 0, 0)),
        out_shape=jax.ShapeDtypeStruct((b, n, 128), BF16),
        compiler_params=pltpu.CompilerParams(dimension_semantics=("arbitrary",), vmem_limit_bytes=VMEM_LIMIT),
        name="nsa_compress",
    )(kvc3, pe2, w_top, w_bot)


def _flash_init(width):
    return (jnp.full((TQ, 1), NEG_INF, F32), jnp.zeros((TQ, 1), F32), jnp.zeros((TQ, width), F32))


def _flash_step(carry, s, mask, v):
    m, l, acc = carry
    if mask is not None:
        s = jnp.where(mask, s, NEG_INF)
    m_new = jnp.maximum(m, jnp.max(s, axis=-1, keepdims=True))
    alpha = jnp.exp(m - m_new)
    p = jnp.exp(s - m_new)
    if mask is not None:
        p = jnp.where(mask, p, 0.0)
    l = alpha * l + jnp.sum(p, axis=-1, keepdims=True)
    acc = alpha * acc + jnp.dot(p.astype(BF16), v, preferred_element_type=F32)
    return m_new, l, acc


def _flash_out(carry):
    _, l, acc = carry
    return acc / jnp.maximum(l, 1e-30)


def _merge_heads(outs):
    lane = lax.broadcasted_iota(jnp.int32, outs[0].shape, 1)
    res = jnp.zeros_like(outs[0])
    for h, o in enumerate(outs):
        res = jnp.where((lane >> 6) == h, o, res)
    return res


def _positions(i, c):
    t_pos = i * TQ + lax.broadcasted_iota(jnp.int32, (TQ, TK), 0)
    k_pos = c * TK + lax.broadcasted_iota(jnp.int32, (TQ, TK), 1)
    return t_pos, k_pos


def _mla_kernel(q_ref, k_ref, v_ref, o_ref):
    i = pl.program_id(1)
    outs = []
    for h in range(N_HEADS):
        q = q_ref[:, h * 128:(h + 1) * 128]

        def chunk(c, carry, masked, q=q, h=h):
            k = k_ref[pl.ds(pl.multiple_of(c * TK, TK), TK), h * 128:(h + 1) * 128]
            v = v_ref[pl.ds(pl.multiple_of(c * TK, TK), TK), :]
            s = _nt_dot(q, k)
            mask = None
            if masked:
                t_pos, k_pos = _positions(i, c)
                mask = k_pos <= t_pos
            return _flash_step(carry, s, mask, v)

        carry = lax.fori_loop(0, i, functools.partial(chunk, masked=False), _flash_init(256))
        carry = chunk(i, carry, True)
        outs.append(_flash_out(carry))
    o_ref[...] = _merge_heads(outs).astype(BF16)


def _mla(a, batch, seq):
    nq = seq // TQ
    return pl.pallas_call(
        _mla_kernel,
        grid=(batch, nq),
        in_specs=[pl.BlockSpec((TQ, 512), lambda b, i: (b * nq + i, 0)),
                  pl.BlockSpec((seq, 512), lambda b, i: (b, 1)),
                  pl.BlockSpec((seq, 256), lambda b, i: (b, 4))],
        out_specs=pl.BlockSpec((TQ, 256), lambda b, i: (b * nq + i, 0)),
        out_shape=jax.ShapeDtypeStruct((batch * seq, 256), BF16),
        compiler_params=pltpu.CompilerParams(dimension_semantics=("arbitrary", "arbitrary"),
                                             vmem_limit_bytes=VMEM_LIMIT),
        name="mla_attn",
    )(a, a, a)


def _fox_kernel(n_chunks, q_ref, k_ref, v_ref, cq_ref, call_ref, hm_ref, eye_ref, o_ref, ct_ref):
    i = pl.program_id(1)

    @pl.when(i == 0)
    def _():
        ct = lax.dot_general(eye_ref[...], call_ref[...], (((1,), (1,)), ((), ())),
                             preferred_element_type=F32, precision=lax.Precision.HIGHEST)
        for c in range(n_chunks):
            ct_ref[c] = ct[0:8, c * TK:(c + 1) * TK]

    q_all = q_ref[...]
    cq = cq_ref[...]
    outs = []
    for h in range(N_HEADS):
        q = q_all * hm_ref[h:h + 1, :]
        c_t = cq[:, h:h + 1]

        def chunk(c, carry, masked, q=q, c_t=c_t, h=h):
            k = k_ref[pl.ds(pl.multiple_of(c * TK, TK), TK), :]
            v = v_ref[pl.ds(pl.multiple_of(c * TK, TK), TK), :]
            s = _nt_dot(q, k) + (c_t - ct_ref[c][h:h + 1, :])
            mask = None
            if masked:
                t_pos, k_pos = _positions(i, c)
                mask = k_pos <= t_pos
            return _flash_step(carry, s, mask, v)

        carry = lax.fori_loop(0, i, functools.partial(chunk, masked=False), _flash_init(256))
        carry = chunk(i, carry, True)
        outs.append(_flash_out(carry))
    o_ref[...] = _merge_heads(outs).astype(BF16)


def _fox(e, d, head_mask, eye, batch, seq):
    nq = seq // TQ
    nc = seq // TK
    return pl.pallas_call(
        functools.partial(_fox_kernel, nc),
        grid=(batch, nq),
        in_specs=[pl.BlockSpec((TQ, 256), lambda b, i: (b * nq + i, 0)),
                  pl.BlockSpec((seq, 256), lambda b, i: (b, 1)),
                  pl.BlockSpec((seq, 256), lambda b, i: (b, 2)),
                  pl.BlockSpec((TQ, 128), lambda b, i: (b * nq + i, 2)),
                  pl.BlockSpec((seq, 128), lambda b, i: (b, 2)),
                  pl.BlockSpec((N_HEADS, 256), lambda b, i: (0, 0)),
                  pl.BlockSpec((128, 128), lambda b, i: (0, 0))],
        out_specs=pl.BlockSpec((TQ, 256), lambda b, i: (b * nq + i, 0)),
        out_shape=jax.ShapeDtypeStruct((batch * seq, 256), BF16),
        scratch_shapes=[pltpu.VMEM((nc, 8, TK), F32)],
        compiler_params=pltpu.CompilerParams(dimension_semantics=("arbitrary", "arbitrary"),
                                             vmem_limit_bytes=VMEM_LIMIT),
        name="fox_attn",
    )(e, e, e, d, d, head_mask, eye)


def _nsa_kernel(n_sel_blocks, q_ref, kvs_ref, kvw_ref, kvc_ref, g_ref, hm_ref, rk_ref, rv_ref, ov_ref,
                o_ref, ks_ref, vs_ref, kw_ref, vw_ref, kc_ref, vc_ref):
    i = pl.program_id(1)

    @pl.when(i == 0)
    def _():
        rk, rv = rk_ref[...], rv_ref[...]
        ks_ref[...] = jnp.dot(kvs_ref[...], rk, preferred_element_type=F32).astype(BF16)
        vs_ref[...] = jnp.dot(kvs_ref[...], rv, preferred_element_type=F32).astype(BF16)
        kw_ref[...] = jnp.dot(kvw_ref[...], rk, preferred_element_type=F32).astype(BF16)
        vw_ref[...] = jnp.dot(kvw_ref[...], rv, preferred_element_type=F32).astype(BF16)
        kc_ref[...] = jnp.dot(kvc_ref[0], rk, preferred_element_type=F32).astype(BF16)
        vc_ref[...] = jnp.dot(kvc_ref[0], rv, preferred_element_type=F32).astype(BF16)

    q_all = q_ref[...]
    g = g_ref[...]
    qs = [q_all * hm_ref[h:h + 1, :] for h in range(N_HEADS)]
    n_cmp_rows = kc_ref.shape[0]

    t_c = i * TQ + lax.broadcasted_iota(jnp.int32, (TQ, n_cmp_rows), 0)
    c_idx = lax.broadcasted_iota(jnp.int32, (TQ, n_cmp_rows), 1)
    cmp_mask = c_idx * NSA_CMP_STRIDE + (NSA_CMP_LEN - 1) <= t_c
    p_sum = jnp.zeros((TQ, n_cmp_rows), F32)
    o_cmp = []
    for h in range(N_HEADS):
        s = jnp.where(cmp_mask, _nt_dot(qs[h], kc_ref[...]), NEG_INF)
        m = jnp.max(s, axis=-1, keepdims=True)
        p = jnp.where(cmp_mask, jnp.exp(s - m), 0.0)
        p = p / jnp.maximum(jnp.sum(p, axis=-1, keepdims=True), 1e-30)
        p_sum = p_sum + p
        o_cmp.append(jnp.dot(p.astype(BF16), vc_ref[...], preferred_element_type=F32))

    imp = jnp.dot(p_sum, ov_ref[...], preferred_element_type=F32, precision=lax.Precision.HIGHEST)
    t_r = i * TQ + lax.broadcasted_iota(jnp.int32, (TQ, LANES), 0)
    j = lax.broadcasted_iota(jnp.int32, (TQ, LANES), 1)
    t_blk = t_r >> 6
    forced = (j == 0) | (j == t_blk) | (j == t_blk - 1)
    imp = jnp.where(forced, NSA_FORCED_SCORE, imp)
    imp = jnp.where(j * NSA_SEL_LEN <= t_r, imp, NEG_INF)
    imp = jnp.where(j < n_sel_blocks, imp, -3e38)
    rank = jnp.zeros((TQ, LANES), F32)
    for i2 in range(n_sel_blocks):
        col = imp[:, i2:i2 + 1]
        beats = (col > imp) | ((col == imp) & (j > i2))
        rank = rank + jnp.where(beats, 1.0, 0.0)
    member = jnp.where((rank < float(NSA_TOP_N)) & (j < n_sel_blocks), 1.0, 0.0).astype(BF16)

    def sel_chunk(c, carries):
        t_pos, k_pos = _positions(i, c)
        jj = lax.broadcasted_iota(jnp.int32, (LANES, TK), 0)
        kk = c * TK + lax.broadcasted_iota(jnp.int32, (LANES, TK), 1)
        expand = jnp.where((kk >> 6) == jj, 1.0, 0.0).astype(BF16)
        in_sel = jnp.dot(member, expand, preferred_element_type=F32) > 0.5
        mask = in_sel & (k_pos <= t_pos)
        k = ks_ref[pl.ds(pl.multiple_of(c * TK, TK), TK), :]
        v = vs_ref[pl.ds(pl.multiple_of(c * TK, TK), TK), :]
        return tuple(_flash_step(carries[h], _nt_dot(qs[h], k), mask, v) for h in range(N_HEADS))

    sel = lax.fori_loop(0, i + 1, sel_chunk, tuple(_flash_init(256) for _ in range(N_HEADS)))

    def win_chunk(c, carries):
        t_pos, k_pos = _positions(i, c)
        mask = (k_pos <= t_pos) & (k_pos > t_pos - NSA_WINDOW)
        k = kw_ref[pl.ds(pl.multiple_of(c * TK, TK), TK), :]
        v = vw_ref[pl.ds(pl.multiple_of(c * TK, TK), TK), :]
        return tuple(_flash_step(carries[h], _nt_dot(qs[h], k), mask, v) for h in range(N_HEADS))

    lo = jnp.maximum(i - NSA_WINDOW // TK, 0)
    win = lax.fori_loop(lo, i + 1, win_chunk, tuple(_flash_init(256) for _ in range(N_HEADS)))

    outs = []
    for h in range(N_HEADS):
        outs.append(g[:, h:h + 1] * o_cmp[h] + g[:, 4 + h:5 + h] * _flash_out(sel[h])
                    + g[:, 8 + h:9 + h] * _flash_out(win[h]))
    o_ref[...] = _merge_heads(outs).astype(BF16)


def _nsa(bm, kvcmp, d, head_mask, rk, rv, overlap, batch, seq):
    nq = seq // TQ
    n_cmp_rows = kvcmp.shape[1]
    const = lambda shape: pl.BlockSpec(shape, lambda b, i: (0, 0))
    return pl.pallas_call(
        functools.partial(_nsa_kernel, seq // NSA_SEL_LEN),
        grid=(batch, nq),
        in_specs=[pl.BlockSpec((TQ, 256), lambda b, i: (b * nq + i, 0)),
                  pl.BlockSpec((seq, 128), lambda b, i: (b, 4)),
                  pl.BlockSpec((seq, 128), lambda b, i: (b, 5)),
                  pl.BlockSpec((1, n_cmp_rows, 128), lambda b, i: (b, 0, 0)),
                  pl.BlockSpec((TQ, 128), lambda b, i: (b * nq + i, 0)),
                  const((N_HEADS, 256)), const((128, 256)), const((128, 256)), const((n_cmp_rows, 128))],
        out_specs=pl.BlockSpec((TQ, 256), lambda b, i: (b * nq + i, 0)),
        out_shape=jax.ShapeDtypeStruct((batch * seq, 256), BF16),
        scratch_shapes=[pltpu.VMEM((seq, 256), BF16)] * 4 + [pltpu.VMEM((n_cmp_rows, 256), BF16)] * 2,
        compiler_params=pltpu.CompilerParams(dimension_semantics=("arbitrary", "arbitrary"),
                                             vmem_limit_bytes=VMEM_LIMIT),
        name="nsa_attn",
    )(bm, bm, bm, kvcmp, d, head_mask, rk, rv, overlap)


INT_MIN = -2 ** 31


def _dsa_kernel(n_keep, idx_bits, q_ref, kvd_ref, qi_ref, ki_ref, w_ref, hm_ref, im_ref, rk_ref, rv_ref,
                o_ref, kd_ref, vd_ref, key_ref):
    i = pl.program_id(1)

    @pl.when(i == 0)
    def _():
        kd_ref[...] = jnp.dot(kvd_ref[...], rk_ref[...], preferred_element_type=F32).astype(BF16)
        vd_ref[...] = jnp.dot(kvd_ref[...], rv_ref[...], preferred_element_type=F32).astype(BF16)

    qi_all = qi_ref[...]
    w = w_ref[...]
    qis = [qi_all * im_ref[h:h + 1, :] for h in range(IDX_HEADS)]

    def score_chunk(c, _):
        ki = ki_ref[pl.ds(pl.multiple_of(c * TK, TK), TK), :]
        score = jnp.zeros((TQ, TK), F32)
        for h in range(IDX_HEADS):
            score = score + jnp.maximum(_nt_dot(qis[h], ki), 0.0) * w[:, h:h + 1]
        t_pos, k_pos = _positions(i, c)
        score = jnp.where(score == 0.0, 0.0, score)
        score = jnp.where(k_pos <= t_pos, score, NEG_INF)
        bits = lax.bitcast_convert_type(score, jnp.int32)
        key_ref[c] = jnp.where(bits < 0, bits ^ jnp.int32(0x7FFFFFFF), bits)
        return 0

    lax.fori_loop(0, i + 1, score_chunk, 0)

    def count(pred):
        def body(c, acc):
            return acc + jnp.sum(jnp.where(pred(key_ref[c], c), 1.0, 0.0), axis=-1, keepdims=True)
        return lax.fori_loop(0, i + 1, body, jnp.zeros((TQ, 1), F32))

    keep = float(n_keep)
    cnt0 = count(lambda k, c: k >= 0)
    thr = jnp.where(cnt0 >= keep, jnp.int32(0), jnp.int32(INT_MIN))

    def thr_bit(it, thr):
        cand = thr | jnp.left_shift(jnp.int32(1), 30 - it)
        cnt = count(lambda k, c: k >= cand)
        return jnp.where(cnt >= keep, cand, thr)

    thr = lax.fori_loop(0, 31, thr_bit, thr)
    need = keep - count(lambda k, c: k > thr)

    def key_index(c):
        return c * TK + lax.broadcasted_iota(jnp.int32, (TQ, TK), 1)

    def cut_bit(it, cut):
        cand = cut | jnp.left_shift(jnp.int32(1), idx_bits - 1 - it)
        cnt = count(lambda k, c: (k == thr) & (key_index(c) < cand))
        return jnp.where(cnt < need, cand, cut)

    cut = lax.fori_loop(0, idx_bits, cut_bit, jnp.zeros((TQ, 1), jnp.int32))

    q_all = q_ref[...]
    qs = [q_all * hm_ref[h:h + 1, :] for h in range(N_HEADS)]

    def attn_chunk(c, carries):
        t_pos, k_pos = _positions(i, c)
        keys = key_ref[c]
        mask = ((keys > thr) | ((keys == thr) & (k_pos <= cut))) & (k_pos <= t_pos)
        k = kd_ref[pl.ds(pl.multiple_of(c * TK, TK), TK), :]
        v = vd_ref[pl.ds(pl.multiple_of(c * TK, TK), TK), :]
        return tuple(_flash_step(carries[h], _nt_dot(qs[h], k), mask, v) for h in range(N_HEADS))

    res = lax.fori_loop(0, i + 1, attn_chunk, tuple(_flash_init(256) for _ in range(N_HEADS)))
    o_ref[...] = _merge_heads([_flash_out(r) for r in res]).astype(BF16)


def _dsa(bm, cidx, d, head_mask, idx_mask, rk, rv, batch, seq):
    nq = seq // TQ
    nc = seq // TK
    n_keep = min(DSA_TOP_K, seq // DSA_KEEP_DIV)
    assert n_keep <= TK
    idx_bits = int(np.log2(seq))
    assert 2 ** idx_bits == seq
    const = lambda shape: pl.BlockSpec(shape, lambda b, i: (0, 0))
    return pl.pallas_call(
        functools.partial(_dsa_kernel, n_keep, idx_bits),
        grid=(batch, nq),
        in_specs=[pl.BlockSpec((TQ, 256), lambda b, i: (b * nq + i, 1)),
                  pl.BlockSpec((seq, 128), lambda b, i: (b, 6)),
                  pl.BlockSpec((TQ, 256), lambda b, i: (b * nq + i, 0)),
                  pl.BlockSpec((seq, 256), lambda b, i: (b, 1)),
                  pl.BlockSpec((TQ, 128), lambda b, i: (b * nq + i, 1)),
                  const((N_HEADS, 256)), const((IDX_HEADS, 256)), const((128, 256)), const((128, 256))],
        out_specs=pl.BlockSpec((TQ, 256), lambda b, i: (b * nq + i, 0)),
        out_shape=jax.ShapeDtypeStruct((batch * seq, 256), BF16),
        scratch_shapes=[pltpu.VMEM((seq, 256), BF16)] * 2 + [pltpu.VMEM((nc, TQ, TK), jnp.int32)],
        compiler_params=pltpu.CompilerParams(dimension_semantics=("arbitrary", "arbitrary"),
                                             vmem_limit_bytes=VMEM_LIMIT),
        name="dsa_attn",
    )(bm, bm, cidx, cidx, d, head_mask, idx_mask, rk, rv)


def _mix_kernel(x_ref, g1_ref, wg_ref, o0_ref, o1_ref, o2_ref, o3_ref, wb_ref, wo_ref, y_ref):
    x = x_ref[...]
    hb = _rms(x, g1_ref[...]).astype(BF16)
    mixed = jnp.zeros((TM, D_MODEL), F32)
    for n, o_ref in enumerate((o0_ref, o1_ref, o2_ref, o3_ref)):
        zg = jnp.dot(hb, wg_ref[:, n * D_MODEL:(n + 1) * D_MODEL], preferred_element_type=F32)
        gate = 1.0 / (1.0 + jnp.exp(-zg))
        lifted = jnp.dot(o_ref[...], wb_ref[n], preferred_element_type=F32)
        mixed = mixed + gate * lifted
    y_ref[...] = x + jnp.dot(mixed.astype(BF16), wo_ref[...], preferred_element_type=F32)


def _mix(x2, g1, wg, outs, wb, wo):
    t = x2.shape[0]
    rows = lambda w: pl.BlockSpec((TM, w), lambda i: (i, 0))
    return pl.pallas_call(
        _mix_kernel,
        grid=(t // TM,),
        in_specs=[rows(D_MODEL), pl.BlockSpec((1, D_MODEL), lambda i: (0, 0)),
                  pl.BlockSpec((D_MODEL, 4 * D_MODEL), lambda i: (0, 0)),
                  rows(256), rows(256), rows(256), rows(256),
                  pl.BlockSpec((4, 256, D_MODEL), lambda i: (0, 0, 0)),
                  pl.BlockSpec((D_MODEL, D_MODEL), lambda i: (0, 0))],
        out_specs=rows(D_MODEL),
        out_shape=jax.ShapeDtypeStruct((t, D_MODEL), F32),
        compiler_params=pltpu.CompilerParams(dimension_semantics=("arbitrary",), vmem_limit_bytes=VMEM_LIMIT),
        name="gate_mix",
    )(x2, g1, wg, *outs, wb, wo)


def _ffn_kernel(final, x_ref, g2_ref, wu_ref, wd_ref, gf_ref, y_ref):
    x = x_ref[...]
    hb = _rms(x, g2_ref[...]).astype(BF16)
    acc = jnp.zeros((TM, D_MODEL), F32)
    for jc in range(D_FF // D_MODEL):
        u = jnp.dot(hb, wu_ref[:, jc * D_MODEL:(jc + 1) * D_MODEL], preferred_element_type=F32)
        u = jnp.square(jnp.maximum(u, 0.0)).astype(BF16)
        acc = acc + jnp.dot(u, wd_ref[jc * D_MODEL:(jc + 1) * D_MODEL, :], preferred_element_type=F32)
    y = x + acc
    if final:
        y = _rms(y, gf_ref[...])
    y_ref[...] = y


def _ffn(x2, g2, wu, wd, gf, final):
    t = x2.shape[0]
    rows = pl.BlockSpec((TM, D_MODEL), lambda i: (i, 0))
    vec = pl.BlockSpec((1, D_MODEL), lambda i: (0, 0))
    return pl.pallas_call(
        functools.partial(_ffn_kernel, final),
        grid=(t // TM,),
        in_specs=[rows, vec, pl.BlockSpec((D_MODEL, D_FF), lambda i: (0, 0)),
                  pl.BlockSpec((D_FF, D_MODEL), lambda i: (0, 0)), vec],
        out_specs=rows,
        out_shape=jax.ShapeDtypeStruct((t, D_MODEL), F32),
        compiler_params=pltpu.CompilerParams(dimension_semantics=("arbitrary",), vmem_limit_bytes=VMEM_LIMIT),
        name="ffn",
    )(x2, g2, wu, wd, gf)


def _compress_weights(cmp_w, cmp_pe):
    wk = cmp_w[0].reshape(NSA_CMP_LEN, HEAD_DIM, HEAD_DIM)
    wv = cmp_w[1].reshape(NSA_CMP_LEN, HEAD_DIM, HEAD_DIM)
    zero = jnp.zeros_like(wk)
    full = jnp.concatenate([jnp.concatenate([wk, zero], axis=2), jnp.concatenate([zero, wv], axis=2)], axis=1)
    half = NSA_CMP_LEN // 2
    w_top = full[:half].reshape(half * 128, 128).astype(BF16)
    w_bot = full[half:].reshape(half * 128, 128).astype(BF16)
    pe = jnp.concatenate([cmp_pe[0], cmp_pe[1]], axis=1)
    pe2 = pe.reshape(2, half * 128)
    return w_top, w_bot, pe2


def _overlap_matrix(seq):
    n_rows = seq // NSA_CMP_STRIDE
    cs = np.arange(n_rows) * NSA_CMP_STRIDE
    sb = np.arange(LANES) * NSA_SEL_LEN
    ov = np.maximum(np.minimum(cs[:, None] + NSA_CMP_LEN, sb[None, :] + NSA_SEL_LEN)
                    - np.maximum(cs[:, None], sb[None, :]), 0).astype(np.float32) / NSA_CMP_LEN
    ov[:, seq // NSA_SEL_LEN:] = 0.0
    ov[(seq - NSA_CMP_LEN) // NSA_CMP_STRIDE + 1:, :] = 0.0
    return jnp.asarray(ov)


def kernel(x, positions, norm1_g, w_in, mla_q_norm_g, mla_w_uq, mla_kv_norm_g, mla_w_ukv, nsa_cmp_pe, nsa_cmp_w,
           fox_f_bias, w_branch, w_out, norm2_g, w_up, w_down, final_g):
    batch, seq, _ = x.shape
    depth = w_in.shape[0]
    assert seq % TQ == 0 and seq % TM == 0 and seq // NSA_SEL_LEN <= LANES
    assert seq // NSA_CMP_STRIDE <= LANES and (seq // NSA_CMP_STRIDE) % 8 == 0

    src, scale = _inproj_layout()
    q_src, k_src, v_src = _mla_up_layouts()
    tab = _rope_tables(positions)
    head_mask = _head_masks(N_HEADS, 256)
    idx_mask = _head_masks(IDX_HEADS, 256)
    rk, rv = _rep_mats()
    eye = jnp.eye(128, dtype=F32)
    overlap = _overlap_matrix(seq)

    x2 = x.reshape(batch * seq, D_MODEL)
    for l in range(depth):
        w_packed = _pack_cols(w_in[l, :, :GATE_BASE], src, scale).astype(BF16)
        wg = w_in[l, :, GATE_BASE:].astype(BF16)
        wuq = _pack_cols(mla_w_uq[l], q_src, np.full(512, (MLA_NOPE + MLA_ROPE) ** -0.5, np.float32)).astype(BF16)
        wuk = _pack_cols(mla_w_ukv[l], k_src, np.ones(512, np.float32)).astype(BF16)
        wuv = jnp.take(mla_w_ukv[l], jnp.asarray(v_src), axis=1).astype(BF16)
        fb = jnp.zeros((1, 128), F32).at[0, :N_HEADS].set(fox_f_bias[l])
        g1 = norm1_g[l][None, :]

        a, bm, kvc, cidx, d, e = _inproj(x2, g1, w_packed, tab, mla_q_norm_g[l][None, :], wuq,
                                          mla_kv_norm_g[l][None, :], wuk, wuv, fb, seq)
        w_top, w_bot, pe2 = _compress_weights(nsa_cmp_w[l], nsa_cmp_pe[l])
        kvcmp = _compress(kvc.reshape(batch, seq // NSA_CMP_STRIDE, NSA_CMP_STRIDE * 128), pe2, w_top, w_bot)

        o_mla = _mla(a, batch, seq)
        o_nsa = _nsa(bm, kvcmp, d, head_mask, rk, rv, overlap, batch, seq)
        o_fox = _fox(e, d, head_mask, eye, batch, seq)
        o_dsa = _dsa(bm, cidx, d, head_mask, idx_mask, rk, rv, batch, seq)

        x2 = _mix(x2, g1, wg, (o_mla, o_nsa, o_fox, o_dsa), w_branch[l].astype(BF16), w_out[l].astype(BF16))
        x2 = _ffn(x2, norm2_g[l][None, :], w_up[l].astype(BF16), w_down[l].astype(BF16), final_g[None, :],
                  l == depth - 1)
    return x2.reshape(batch, seq, D_MODEL)
```

```python
import functools

import numpy as np
import jax
import jax.numpy as jnp
from jax import lax
from jax.experimental import pallas as pl
from jax.experimental.pallas import tpu as pltpu

F32 = jnp.float32
BF16 = jnp.bfloat16

D_MODEL = 1024
ROPE_THETA = 500000.0
NEG_INF = -1e30
NORM_EPS = 1e-6
HEAD_DIM = 64
N_HEADS = 4
D_FF = 4 * D_MODEL
MLA_NOPE, MLA_ROPE, MLA_V = 64, 32, 64
MLA_Q_LORA, MLA_KV_LORA = 256, 128
NSA_CMP_LEN, NSA_CMP_STRIDE, NSA_SEL_LEN = 32, 16, 64
NSA_TOP_N, NSA_WINDOW, NSA_FORCED_SCORE = 8, 512, 1e4
IDX_HEADS, IDX_DIM = 8, 32
DSA_TOP_K, DSA_KEEP_DIV = 256, 4
MLA_COLS, NSA_COLS, FOX_COLS, DSA_COLS = 416, 652, 772, 680
GATE_BASE = MLA_COLS + NSA_COLS + FOX_COLS + DSA_COLS

LANES = 128
TQ = 256
TK = 256
TM = 256
VMEM_LIMIT = 56 * 1024 * 1024
INT_MIN = -2 ** 31

P_CQ, P_CKV, P_KR = 0, 256, 384
P_NQ, P_DQ = 512, 768
P_KVC, P_KVS, P_KVW, P_KVD = 1024, 1152, 1280, 1408
P_QI, P_KI = 1536, 1792
P_G, P_W = 2048, 2176
P_FQ, P_FK, P_FV, P_FF = 2304, 2560, 2816, 3072
P_TOTAL = 3200


def _inproj_layout():
    src = -np.ones(P_TOTAL, np.int64)
    scale = np.ones(P_TOTAL, np.float32)
    mla, nsa, fox, dsa = 0, MLA_COLS, MLA_COLS + NSA_COLS, MLA_COLS + NSA_COLS + FOX_COLS
    ar = np.arange
    src[P_CQ:P_CQ + 256] = mla + ar(256)
    src[P_CKV:P_CKV + 128] = mla + 256 + ar(128)
    src[P_KR + 64:P_KR + 96] = mla + 384 + ar(32)
    src[P_NQ:P_NQ + 256] = nsa + ar(256)
    scale[P_NQ:P_NQ + 256] = HEAD_DIM ** -0.5
    src[P_DQ:P_DQ + 256] = dsa + ar(256)
    scale[P_DQ:P_DQ + 256] = HEAD_DIM ** -0.5
    for base, k_off, v_off, origin in ((P_KVC, 256, 320, nsa), (P_KVS, 384, 448, nsa),
                                       (P_KVW, 512, 576, nsa), (P_KVD, 256, 320, dsa)):
        src[base:base + 64] = origin + k_off + ar(64)
        src[base + 64:base + 128] = origin + v_off + ar(64)
    src[P_QI:P_QI + 256] = dsa + 384 + ar(256)
    scale[P_QI:P_QI + 256] = (IDX_DIM * IDX_HEADS) ** -0.5
    for r in range(IDX_HEADS):
        src[P_KI + 32 * r:P_KI + 32 * r + 32] = dsa + 640 + ar(32)
    src[P_G:P_G + 12] = nsa + 640 + ar(12)
    src[P_W:P_W + 8] = dsa + 672 + ar(8)
    src[P_FQ:P_FQ + 256] = fox + ar(256)
    scale[P_FQ:P_FQ + 256] = HEAD_DIM ** -0.5
    src[P_FK:P_FK + 256] = fox + 256 + ar(256)
    src[P_FV:P_FV + 256] = fox + 512 + ar(256)
    src[P_FF:P_FF + 4] = fox + 768 + ar(4)
    return src, scale


def _pack_cols(w, src, scale):
    cols = jnp.take(w, jnp.asarray(np.maximum(src, 0)), axis=1)
    return cols * jnp.asarray(np.where(src >= 0, scale, 0.0).astype(np.float32))[None, :]


def _mla_up_layouts():
    q_src = -np.ones(512, np.int64)
    k_src = -np.ones(512, np.int64)
    v_src = np.zeros(256, np.int64)
    for h in range(N_HEADS):
        q_src[h * 128:h * 128 + 96] = h * 96 + np.arange(96)
        k_src[h * 128:h * 128 + 64] = h * 128 + np.arange(64)
        v_src[h * 64:h * 64 + 64] = h * 128 + 64 + np.arange(64)
    return q_src, k_src, v_src


def _rope_tables(positions):
    pos = positions.astype(F32).reshape(-1)[:, None]
    n = pos.shape[0]
    parts, offs = [], {}
    col = 0
    for name, rot in (("mla", MLA_ROPE), ("head", HEAD_DIM // 4), ("idx", IDX_DIM // 4)):
        inv = ROPE_THETA ** (-jnp.arange(0, rot, 2, dtype=F32) / rot)
        ang = pos * inv[None, :]
        parts += [jnp.cos(ang), jnp.sin(ang)]
        offs[name] = (col, col + rot // 2, rot // 2)
        col += rot
    one_col = col
    parts.append(jnp.ones((n, 64 - col), F32))
    cs = jnp.concatenate(parts, axis=1)

    place = np.zeros((64, 768), np.float32)

    def fill(cos_base, sin_base, name, group, first, n_groups):
        c0, s0, half = offs[name]
        place[one_col, cos_base:cos_base + 128] = 1.0
        for g in range(n_groups):
            lo = g * group + first
            for j in range(half):
                for lane, sign in ((lo + j, -1.0), (lo + half + j, 1.0)):
                    place[one_col, cos_base + lane] = 0.0
                    place[c0 + j, cos_base + lane] = 1.0
                    place[s0 + j, sin_base + lane] = sign

    fill(0, 128, "mla", 128, 64, 1)
    fill(256, 384, "head", 64, 0, 2)
    fill(512, 640, "idx", 32, 0, 4)
    return jnp.dot(cs, jnp.asarray(place), precision=lax.Precision.HIGHEST)


def _head_masks(n_heads, width):
    lane = np.arange(width)
    return jnp.asarray((lane[None, :] // (width // n_heads) == np.arange(n_heads)[:, None]), BF16)


def _placement_constants(seq):
    d = np.arange(64)
    rk = np.zeros((128, 256), np.float32)
    pvs = np.zeros((128, 128), np.float32)
    pv4 = np.zeros((4, 256, 128), np.float32)
    pt4 = np.zeros((4, 128, 256), np.float32)
    pvs[64 + d, d] = 1.0
    for h in range(N_HEADS):
        rk[d, h * 64 + d] = 1.0
        pv4[h, h * 64 + d, d] = 1.0
        pt4[h, d, h * 64 + d] = 1.0
    one_row = np.zeros((1, 128), np.float32)
    one_row[0, 64] = 1.0
    r, c = np.arange(TQ)[:, None], np.arange(TK)[None, :]
    tril = np.where(c <= r, 0.0, NEG_INF).astype(np.float32)
    triu = np.where(c > r, 0.0, NEG_INF).astype(np.float32)
    nc = seq // TK
    key_blk = (np.arange(seq) // NSA_SEL_LEN).reshape(nc, 1, TK)
    expand = (key_blk == np.arange(LANES).reshape(1, LANES, 1)).astype(np.float32)
    bf = lambda a: jnp.asarray(a, BF16)
    return dict(rk=bf(rk), pvs=bf(pvs), pv4=bf(pv4), pt4=bf(pt4), one_row=jnp.asarray(one_row),
                tril=jnp.asarray(tril), triu=jnp.asarray(triu), expand=bf(expand))


def _rms(x, g):
    return x * lax.rsqrt(jnp.mean(x * x, axis=-1, keepdims=True) + NORM_EPS) * g


def _rope_tile(x, cos, sin, is_x1, half):
    fwd = pltpu.roll(x, LANES - half, axis=1)
    bwd = pltpu.roll(x, half, axis=1)
    return x * cos + jnp.where(is_x1, fwd, bwd) * sin


def _nt_dot(a, b):
    return lax.dot_general(a, b, (((1,), (1,)), ((), ())), preferred_element_type=F32)


def _attn_params():
    return pltpu.CompilerParams(dimension_semantics=("arbitrary", "arbitrary"), vmem_limit_bytes=VMEM_LIMIT)


def _row_params():
    return pltpu.CompilerParams(dimension_semantics=("arbitrary",), vmem_limit_bytes=VMEM_LIMIT)


def _inproj_kernel(tiles_per_seq, x_ref, g1_ref, w_ref, tab_ref, gq_ref, wuq_ref, gkv_ref, wuk_ref,
                   wuv_ref, fb_ref, a_ref, b_ref, kvc_ref, c_ref, d_ref, e_ref, carry_ref):
    i = pl.program_id(0)
    x = x_ref[...]
    hb = _rms(x, g1_ref[...]).astype(BF16)
    z = jnp.dot(hb, w_ref[...], preferred_element_type=F32)

    lane = lax.broadcasted_iota(jnp.int32, (TM, LANES), 1)
    cos1, sin1 = tab_ref[:, 0:128], tab_ref[:, 128:256]
    cos3, sin3 = tab_ref[:, 256:384], tab_ref[:, 384:512]
    cos5, sin5 = tab_ref[:, 512:640], tab_ref[:, 640:768]
    x1_mla = (lane >= 64) & (lane < 80)
    x1_head = (lane & 63) < 8
    x1_idx = (lane & 31) < 4
    k_half = lane < 64
    cos4 = jnp.where(k_half, cos3, 1.0)
    sin4 = jnp.where(k_half, sin3, 0.0)

    cqn = _rms(z[:, P_CQ:P_CQ + 256], gq_ref[...]).astype(BF16)
    q_up = jnp.dot(cqn, wuq_ref[...], preferred_element_type=F32)
    ckvn = _rms(z[:, P_CKV:P_CKV + 128], gkv_ref[...]).astype(BF16)
    k_up = jnp.dot(ckvn, wuk_ref[...], preferred_element_type=F32)
    v_up = jnp.dot(ckvn, wuv_ref[...], preferred_element_type=F32)
    k_rot = _rope_tile(z[:, P_KR:P_KR + 128], cos1, sin1, x1_mla, 16)
    for h in range(N_HEADS):
        sl = slice(h * 128, (h + 1) * 128)
        a_ref[:, sl] = _rope_tile(q_up[:, sl], cos1, sin1, x1_mla, 16).astype(BF16)
        a_ref[:, 512 + h * 128:512 + (h + 1) * 128] = (k_up[:, sl] + k_rot).astype(BF16)
    a_ref[:, 1024:1280] = v_up.astype(BF16)

    for t in range(4):
        b_ref[:, t * 128:(t + 1) * 128] = _rope_tile(
            z[:, P_NQ + t * 128:P_NQ + (t + 1) * 128], cos3, sin3, x1_head, 8).astype(BF16)
    kvc_ref[...] = _rope_tile(z[:, P_KVC:P_KVC + 128], cos4, sin4, x1_head & k_half, 8).astype(BF16)
    for t, off in enumerate((P_KVS, P_KVW, P_KVD)):
        b_ref[:, 512 + t * 128:512 + (t + 1) * 128] = _rope_tile(
            z[:, off:off + 128], cos4, sin4, x1_head & k_half, 8).astype(BF16)

    for t in range(4):
        c_ref[:, t * 128:(t + 1) * 128] = _rope_tile(
            z[:, P_QI + t * 128:P_QI + (t + 1) * 128], cos5, sin5, x1_idx, 4).astype(BF16)

    d_ref[:, 0:128] = 1.0 / (1.0 + jnp.exp(-z[:, P_G:P_G + 128]))
    d_ref[:, 128:256] = z[:, P_W:P_W + 128]
    f = z[:, P_FF:P_FF + 128] + fb_ref[...]
    log_f = jnp.minimum(f, 0.0) - jnp.log(1.0 + jnp.exp(-jnp.abs(f)))

    @pl.when(i % tiles_per_seq == 0)
    def _():
        carry_ref[...] = jnp.zeros_like(carry_ref)

    row = lax.broadcasted_iota(jnp.int32, (TM, TM), 0)
    col = lax.broadcasted_iota(jnp.int32, (TM, TM), 1)
    tri = jnp.where(col <= row, 1.0, 0.0).astype(F32)
    c = jnp.dot(tri, log_f, preferred_element_type=F32, precision=lax.Precision.HIGHEST)
    c = c + carry_ref[0:1, :]
    d_ref[:, 256:384] = c
    carry_ref[...] = jnp.broadcast_to(c[TM - 1:TM, :], carry_ref.shape)

    e_ref[...] = z[:, P_FQ:P_FQ + 768].astype(BF16)


def _inproj(x2, g1, w_packed, tab, gq, wuq, gkv, wuk, wuv, fb, seq):
    t = x2.shape[0]
    full = lambda shape: pl.BlockSpec(shape, lambda i: (0, 0))
    rows = lambda w: pl.BlockSpec((TM, w), lambda i: (i, 0))
    return pl.pallas_call(
        functools.partial(_inproj_kernel, seq // TM),
        grid=(t // TM,),
        in_specs=[rows(D_MODEL), full((1, D_MODEL)), full((D_MODEL, P_TOTAL)), rows(768),
                  full((1, 256)), full((256, 512)), full((1, 128)), full((128, 512)), full((128, 256)),
                  full((1, 128))],
        out_specs=[rows(1280), rows(896), rows(128), rows(512), rows(384), rows(768)],
        out_shape=[jax.ShapeDtypeStruct((t, 1280), BF16), jax.ShapeDtypeStruct((t, 896), BF16),
                   jax.ShapeDtypeStruct((t, 128), BF16), jax.ShapeDtypeStruct((t, 512), BF16),
                   jax.ShapeDtypeStruct((t, 384), F32), jax.ShapeDtypeStruct((t, 768), BF16)],
        scratch_shapes=[pltpu.VMEM((8, 128), F32)],
        compiler_params=_row_params(),
        name="inproj",
    )(x2, g1, w_packed, tab, gq, wuq, gkv, wuk, wuv, fb)


def _compress_kernel(r_ref, pe_ref, wt_ref, wb_ref, o_ref):
    r = r_ref[0].astype(F32)
    top = jnp.dot((r + pe_ref[0:1, :]).astype(BF16), wt_ref[...], preferred_element_type=F32)
    bot = jnp.dot((r + pe_ref[1:2, :]).astype(BF16), wb_ref[...], preferred_element_type=F32)
    n = bot.shape[0]
    o_ref[0] = (top + pltpu.roll(bot, n - 1, axis=0)).astype(BF16)


def _compress(kvc3, pe2, w_top, w_bot):
    b, n, _ = kvc3.shape
    return pl.pallas_call(
        _compress_kernel,
        grid=(b,),
        in_specs=[pl.BlockSpec((1, n, 2048), lambda i: (i, 0, 0)), pl.BlockSpec((2, 2048), lambda i: (0, 0)),
                  pl.BlockSpec((2048, 128), lambda i: (0, 0)), pl.BlockSpec((2048, 128), lambda i: (0, 0))],
        out_specs=pl.BlockSpec((1, n, 128), lambda i: (i, 0, 0)),
        out_shape=jax.ShapeDtypeStruct((b, n, 128), BF16),
        compiler_params=_row_params(),
        name="nsa_compress",
    )(kvc3, pe2, w_top, w_bot)


def _flash_init(rows):
    return jnp.full((rows, 1), NEG_INF, F32), jnp.zeros((rows, LANES), F32)


def _flash_update(carry, s, v):
    m, acc = carry
    m_new = jnp.maximum(m, jnp.max(s, axis=-1, keepdims=True))
    p = jnp.exp(s - m_new).astype(BF16)
    alpha = jnp.exp(m - m_new)
    return m_new, alpha * acc + jnp.dot(p, v, preferred_element_type=F32)


def _flash_out(acc):
    return acc / jnp.maximum(acc[:, 64:65], 1e-30)


def _place_heads(outs, pt_ref):
    res = None
    for h, o in enumerate(outs):
        t = jnp.dot(o.astype(BF16), pt_ref[h], preferred_element_type=F32)
        res = t if res is None else res + t
    return res.astype(BF16)


def _chunk_rows(c):
    return pl.ds(pl.multiple_of(c * TK, TK), TK)


def _stack_heads(x, mask_ref, n):
    return jnp.concatenate([x * mask_ref[h:h + 1, :] for h in range(n)], axis=0)


def _mla_kernel(q_ref, k_ref, v_ref, pv_ref, one_ref, pt_ref, tril_ref, o_ref, vaug_ref):
    i = pl.program_id(1)

    @pl.when(i == 0)
    def _():
        v = v_ref[...]
        for h in range(N_HEADS):
            vaug_ref[h] = (jnp.dot(v, pv_ref[h], preferred_element_type=F32) + one_ref[...]).astype(BF16)

    qs = [q_ref[:, h * 128:(h + 1) * 128] for h in range(N_HEADS)]

    def step(c, carries, bias):
        rows = _chunk_rows(c)
        new = []
        for h in range(N_HEADS):
            s = _nt_dot(qs[h], k_ref[rows, h * 128:(h + 1) * 128])
            if bias is not None:
                s = s + bias
            new.append(_flash_update(carries[h], s, vaug_ref[h, rows, :]))
        return tuple(new)

    carries = lax.fori_loop(0, i, lambda c, cr: step(c, cr, None),
                            tuple(_flash_init(TQ) for _ in range(N_HEADS)))
    carries = step(i, carries, tril_ref[...])
    o_ref[...] = _place_heads([_flash_out(acc) for _, acc in carries], pt_ref)


def _mla(a, k, batch, seq):
    nq = seq // TQ
    const2 = lambda shape: pl.BlockSpec(shape, lambda b, i: (0, 0))
    const3 = lambda shape: pl.BlockSpec(shape, lambda b, i: (0, 0, 0))
    return pl.pallas_call(
        _mla_kernel,
        grid=(batch, nq),
        in_specs=[pl.BlockSpec((TQ, 512), lambda b, i: (b * nq + i, 0)),
                  pl.BlockSpec((seq, 512), lambda b, i: (b, 1)),
                  pl.BlockSpec((seq, 256), lambda b, i: (b, 4)),
                  const3((4, 256, 128)), const2((1, 128)), const3((4, 128, 256)), const2((TQ, TK))],
        out_specs=pl.BlockSpec((TQ, 256), lambda b, i: (b * nq + i, 0)),
        out_shape=jax.ShapeDtypeStruct((batch * seq, 256), BF16),
        scratch_shapes=[pltpu.VMEM((N_HEADS, seq, 128), BF16)],
        compiler_params=_attn_params(),
        name="mla_attn",
    )(a, a, a, k["pv4"], k["one_row"], k["pt4"], k["tril"])


def _fox_kernel(n_chunks, q_ref, k_ref, v_ref, cq_ref, call_ref, hm_ref, eye_ref, pv_ref, one_ref, pt_ref,
                tril_ref, o_ref, vaug_ref, ct_ref):
    i = pl.program_id(1)

    @pl.when(i == 0)
    def _():
        v = v_ref[...]
        for h in range(N_HEADS):
            vaug_ref[h] = (jnp.dot(v, pv_ref[h], preferred_element_type=F32) + one_ref[...]).astype(BF16)
        ct = lax.dot_general(eye_ref[...], call_ref[...], (((1,), (1,)), ((), ())),
                             preferred_element_type=F32, precision=lax.Precision.HIGHEST)
        for c in range(n_chunks):
            ct_ref[c] = ct[0:8, c * TK:(c + 1) * TK]

    q_all = q_ref[...]
    cq = cq_ref[...]
    qs = [q_all * hm_ref[h:h + 1, :] for h in range(N_HEADS)]
    cts = [cq[:, h:h + 1] for h in range(N_HEADS)]

    def step(c, carries, bias):
        rows = _chunk_rows(c)
        k = k_ref[rows, :]
        cs = ct_ref[c]
        new = []
        for h in range(N_HEADS):
            s = _nt_dot(qs[h], k) + (cts[h] - cs[h:h + 1, :])
            if bias is not None:
                s = s + bias
            new.append(_flash_update(carries[h], s, vaug_ref[h, rows, :]))
        return tuple(new)

    carries = lax.fori_loop(0, i, lambda c, cr: step(c, cr, None),
                            tuple(_flash_init(TQ) for _ in range(N_HEADS)))
    carries = step(i, carries, tril_ref[...])
    o_ref[...] = _place_heads([_flash_out(acc) for _, acc in carries], pt_ref)


def _fox(e, d, head_mask, eye, k, batch, seq):
    nq = seq // TQ
    nc = seq // TK
    const2 = lambda shape: pl.BlockSpec(shape, lambda b, i: (0, 0))
    const3 = lambda shape: pl.BlockSpec(shape, lambda b, i: (0, 0, 0))
    return pl.pallas_call(
        functools.partial(_fox_kernel, nc),
        grid=(batch, nq),
        in_specs=[pl.BlockSpec((TQ, 256), lambda b, i: (b * nq + i, 0)),
                  pl.BlockSpec((seq, 256), lambda b, i: (b, 1)),
                  pl.BlockSpec((seq, 256), lambda b, i: (b, 2)),
                  pl.BlockSpec((TQ, 128), lambda b, i: (b * nq + i, 2)),
                  pl.BlockSpec((seq, 128), lambda b, i: (b, 2)),
                  const2((N_HEADS, 256)), const2((128, 128)),
                  const3((4, 256, 128)), const2((1, 128)), const3((4, 128, 256)), const2((TQ, TK))],
        out_specs=pl.BlockSpec((TQ, 256), lambda b, i: (b * nq + i, 0)),
        out_shape=jax.ShapeDtypeStruct((batch * seq, 256), BF16),
        scratch_shapes=[pltpu.VMEM((N_HEADS, seq, 128), BF16), pltpu.VMEM((nc, 8, TK), F32)],
        compiler_params=_attn_params(),
        name="fox_attn",
    )(e, e, e, d, d, head_mask, eye, k["pv4"], k["one_row"], k["pt4"], k["tril"])


def _nsa_kernel(n_sel_blocks, q_ref, kvs_ref, kvw_ref, kvc_ref, g_ref, hm_ref, rk_ref, pvs_ref, one_ref,
                pt_ref, tril_ref, triu_ref, ex_ref, ov_ref, o_ref, ks_ref, vs_ref, kw_ref, vw_ref, kc_ref,
                vc_ref):
    i = pl.program_id(1)

    @pl.when(i == 0)
    def _():
        rk, pvs, one = rk_ref[...], pvs_ref[...], one_ref[...]
        for src_ref, k_dst, v_dst in ((kvs_ref, ks_ref, vs_ref), (kvw_ref, kw_ref, vw_ref)):
            kv = src_ref[...]
            k_dst[...] = jnp.dot(kv, rk, preferred_element_type=F32).astype(BF16)
            v_dst[...] = (jnp.dot(kv, pvs, preferred_element_type=F32) + one).astype(BF16)
        kv = kvc_ref[0]
        kc_ref[...] = jnp.dot(kv, rk, preferred_element_type=F32).astype(BF16)
        vc_ref[...] = jnp.dot(kv, pvs, preferred_element_type=F32).astype(BF16)

    g = g_ref[...]
    q_stack = _stack_heads(q_ref[...], hm_ref, N_HEADS)
    n_cmp_rows = kc_ref.shape[0]
    head = lambda x, h: x[h * TQ:(h + 1) * TQ]

    t_c = i * TQ + lax.broadcasted_iota(jnp.int32, (TQ, n_cmp_rows), 0)
    c_idx = lax.broadcasted_iota(jnp.int32, (TQ, n_cmp_rows), 1)
    cmp_mask = c_idx * NSA_CMP_STRIDE + (NSA_CMP_LEN - 1) <= t_c
    s_cmp = _nt_dot(q_stack, kc_ref[...])
    p_sum = jnp.zeros((TQ, n_cmp_rows), F32)
    o_cmp = []
    for h in range(N_HEADS):
        s = jnp.where(cmp_mask, head(s_cmp, h), NEG_INF)
        m = jnp.max(s, axis=-1, keepdims=True)
        p = jnp.where(cmp_mask, jnp.exp(s - m), 0.0)
        p = p / jnp.maximum(jnp.sum(p, axis=-1, keepdims=True), 1e-30)
        p_sum = p_sum + p
        o_cmp.append(jnp.dot(p.astype(BF16), vc_ref[...], preferred_element_type=F32))

    imp = jnp.dot(p_sum, ov_ref[...], preferred_element_type=F32, precision=lax.Precision.HIGHEST)
    t_r = i * TQ + lax.broadcasted_iota(jnp.int32, (TQ, LANES), 0)
    j = lax.broadcasted_iota(jnp.int32, (TQ, LANES), 1)
    t_blk = t_r >> 6
    forced = (j == 0) | (j == t_blk) | (j == t_blk - 1)
    imp = jnp.where(forced, NSA_FORCED_SCORE, imp)
    imp = jnp.where(j * NSA_SEL_LEN <= t_r, imp, NEG_INF)
    imp = jnp.where(j < n_sel_blocks, imp, -3e38)
    rank = jnp.zeros((TQ, LANES), F32)
    for i2 in range(n_sel_blocks):
        col = imp[:, i2:i2 + 1]
        beats = jnp.where(col > imp, 1.0, jnp.where(col == imp, jnp.where(j > i2, 1.0, 0.0), 0.0))
        rank = rank + beats
    member = jnp.where(rank < float(NSA_TOP_N), jnp.where(j < n_sel_blocks, 1.0, 0.0), 0.0).astype(BF16)

    def sel_step(c, carry, bias):
        rows = _chunk_rows(c)
        in_sel = jnp.dot(member, ex_ref[c], preferred_element_type=F32)
        pen = jnp.where(in_sel > 0.5, 0.0, NEG_INF)
        if bias is not None:
            pen = pen + bias
        s_all = _nt_dot(q_stack, ks_ref[rows, :])
        s = jnp.concatenate([head(s_all, h) + pen for h in range(N_HEADS)], axis=0)
        return _flash_update(carry, s, vs_ref[rows, :])

    sel = lax.fori_loop(0, i, lambda c, cr: sel_step(c, cr, None), _flash_init(N_HEADS * TQ))
    _, sel_acc = sel_step(i, sel, tril_ref[...])

    win = _flash_init(N_HEADS * TQ)
    for back, bias_ref in ((2, triu_ref), (1, None), (0, tril_ref)):
        c = jnp.maximum(i - back, 0)
        rows = _chunk_rows(c)
        void = jnp.where(i >= back, 0.0, NEG_INF)
        pen = void if bias_ref is None else bias_ref[...] + void
        s_all = _nt_dot(q_stack, kw_ref[rows, :])
        s = jnp.concatenate([head(s_all, h) + pen for h in range(N_HEADS)], axis=0)
        win = _flash_update(win, s, vw_ref[rows, :])
    _, win_acc = win

    o_sel = _flash_out(sel_acc)
    o_win = _flash_out(win_acc)
    outs = []
    for h in range(N_HEADS):
        outs.append(g[:, h:h + 1] * o_cmp[h] + g[:, 4 + h:5 + h] * head(o_sel, h)
                    + g[:, 8 + h:9 + h] * head(o_win, h))
    o_ref[...] = _place_heads(outs, pt_ref)


def _nsa(bm, kvcmp, d, head_mask, overlap, k, batch, seq):
    nq = seq // TQ
    nc = seq // TK
    n_cmp_rows = kvcmp.shape[1]
    const2 = lambda shape: pl.BlockSpec(shape, lambda b, i: (0, 0))
    const3 = lambda shape: pl.BlockSpec(shape, lambda b, i: (0, 0, 0))
    return pl.pallas_call(
        functools.partial(_nsa_kernel, seq // NSA_SEL_LEN),
        grid=(batch, nq),
        in_specs=[pl.BlockSpec((TQ, 256), lambda b, i: (b * nq + i, 0)),
                  pl.BlockSpec((seq, 128), lambda b, i: (b, 4)),
                  pl.BlockSpec((seq, 128), lambda b, i: (b, 5)),
                  pl.BlockSpec((1, n_cmp_rows, 128), lambda b, i: (b, 0, 0)),
                  pl.BlockSpec((TQ, 128), lambda b, i: (b * nq + i, 0)),
                  const2((N_HEADS, 256)), const2((128, 256)), const2((128, 128)), const2((1, 128)),
                  const3((4, 128, 256)), const2((TQ, TK)), const2((TQ, TK)), const3((nc, LANES, TK)),
                  const2((n_cmp_rows, 128))],
        out_specs=pl.BlockSpec((TQ, 256), lambda b, i: (b * nq + i, 0)),
        out_shape=jax.ShapeDtypeStruct((batch * seq, 256), BF16),
        scratch_shapes=[pltpu.VMEM((seq, 256), BF16), pltpu.VMEM((seq, 128), BF16),
                        pltpu.VMEM((seq, 256), BF16), pltpu.VMEM((seq, 128), BF16),
                        pltpu.VMEM((n_cmp_rows, 256), BF16), pltpu.VMEM((n_cmp_rows, 128), BF16)],
        compiler_params=_attn_params(),
        name="nsa_attn",
    )(bm, bm, bm, kvcmp, d, head_mask, k["rk"], k["pvs"], k["one_row"], k["pt4"], k["tril"], k["triu"],
      k["expand"], overlap)


def _dsa_kernel(n_keep, idx_bits, q_ref, kvd_ref, qi_ref, ki_ref, w_ref, hm_ref, im_ref, rk_ref, pvs_ref,
                one_ref, pt_ref, tril_ref, o_ref, kd_ref, vd_ref, sc_ref):
    i = pl.program_id(1)

    @pl.when(i == 0)
    def _():
        kv = kvd_ref[...]
        kd_ref[...] = jnp.dot(kv, rk_ref[...], preferred_element_type=F32).astype(BF16)
        vd_ref[...] = (jnp.dot(kv, pvs_ref[...], preferred_element_type=F32) + one_ref[...]).astype(BF16)

    head = lambda x, h: x[h * TQ:(h + 1) * TQ]

    qi_stack = _stack_heads(qi_ref[...], im_ref, IDX_HEADS)
    w = w_ref[...]
    ws = [w[:, h:h + 1] for h in range(IDX_HEADS)]

    def score_chunk(c, bias):
        logits = _nt_dot(qi_stack, ki_ref[_chunk_rows(c), :])
        score = jnp.maximum(head(logits, 0), 0.0) * ws[0]
        for h in range(1, IDX_HEADS):
            score = score + jnp.maximum(head(logits, h), 0.0) * ws[h]
        score = jnp.where(score == 0.0, 0.0, score)
        if bias is not None:
            score = score + bias
        sc_ref[c] = score

    def score_body(c, carry):
        score_chunk(c, None)
        return carry

    lax.fori_loop(0, i, score_body, 0)
    score_chunk(i, tril_ref[...])

    def count(pred):
        def body(c, part):
            hit = jnp.where(pred(sc_ref[c], c), 1.0, 0.0)
            return part + (hit[:, :LANES] + hit[:, LANES:])
        part = lax.fori_loop(0, i + 1, body, jnp.zeros((TQ, LANES), F32))
        return jnp.sum(part, axis=-1, keepdims=True)

    def key_to_float(key):
        bits = jnp.where(key < 0, key ^ jnp.int32(0x7FFFFFFF), key)
        return lax.bitcast_convert_type(bits, F32)

    keep = float(n_keep)
    total = ((i + 1) * TK).astype(F32)
    cnt0 = count(lambda s, c: s >= 0.0)
    thr0 = jnp.where(cnt0 >= keep, jnp.int32(0), jnp.int32(INT_MIN))
    cnt0 = jnp.where(cnt0 >= keep, cnt0, total)

    def thr_bit(it, carry):
        thr, cnt_thr = carry
        cand = thr | jnp.left_shift(jnp.int32(1), 30 - it)
        cand_f = key_to_float(cand)
        cnt = count(lambda s, c: s >= cand_f)
        take = cnt >= keep
        return jnp.where(take, cand, thr), jnp.where(take, cnt, cnt_thr)

    thr, cnt_thr = lax.fori_loop(0, 31, thr_bit, (thr0, cnt0))
    thr_f = key_to_float(thr)

    def key_index(c):
        return c * TK + lax.broadcasted_iota(jnp.int32, (TQ, TK), 1)

    def resolve_ties(_):
        need = keep - count(lambda s, c: s > thr_f)

        def cut_bit(it, cut):
            cand = cut | jnp.left_shift(jnp.int32(1), idx_bits - 1 - it)
            cnt = count(lambda s, c: (s == thr_f) & (key_index(c) < cand))
            return jnp.where(cnt < need, cand, cut)

        return lax.fori_loop(0, idx_bits, cut_bit, jnp.zeros((TQ, 1), jnp.int32))

    has_excess = jnp.max(jnp.abs(cnt_thr - keep)) > 0.0
    cut = lax.cond(has_excess, resolve_ties, lambda _: jnp.full((TQ, 1), 2 ** idx_bits, jnp.int32), 0)

    q_stack = _stack_heads(q_ref[...], hm_ref, N_HEADS)

    def attn_step(c, carry, bias):
        rows = _chunk_rows(c)
        sc = sc_ref[c]
        tie = jnp.where(key_index(c) <= cut, 0.0, NEG_INF)
        pen = jnp.where(sc > thr_f, 0.0, jnp.where(sc == thr_f, tie, NEG_INF))
        if bias is not None:
            pen = pen + bias
        s_all = _nt_dot(q_stack, kd_ref[rows, :])
        s = jnp.concatenate([head(s_all, h) + pen for h in range(N_HEADS)], axis=0)
        return _flash_update(carry, s, vd_ref[rows, :])

    carry = lax.fori_loop(0, i, lambda c, cr: attn_step(c, cr, None), _flash_init(N_HEADS * TQ))
    _, acc = attn_step(i, carry, tril_ref[...])
    out = _flash_out(acc)
    o_ref[...] = _place_heads([head(out, h) for h in range(N_HEADS)], pt_ref)


def _dsa(bm, cidx, d, head_mask, idx_mask, k, batch, seq):
    nq = seq // TQ
    nc = seq // TK
    n_keep = min(DSA_TOP_K, seq // DSA_KEEP_DIV)
    assert n_keep <= TK
    idx_bits = int(np.log2(seq))
    assert 2 ** idx_bits == seq
    const2 = lambda shape: pl.BlockSpec(shape, lambda b, i: (0, 0))
    const3 = lambda shape: pl.BlockSpec(shape, lambda b, i: (0, 0, 0))
    return pl.pallas_call(
        functools.partial(_dsa_kernel, n_keep, idx_bits),
        grid=(batch, nq),
        in_specs=[pl.BlockSpec((TQ, 256), lambda b, i: (b * nq + i, 1)),
                  pl.BlockSpec((seq, 128), lambda b, i: (b, 6)),
                  pl.BlockSpec((TQ, 256), lambda b, i: (b * nq + i, 0)),
                  pl.BlockSpec((seq, 256), lambda b, i: (b, 1)),
                  pl.BlockSpec((TQ, 128), lambda b, i: (b * nq + i, 1)),
                  const2((N_HEADS, 256)), const2((IDX_HEADS, 256)), const2((128, 256)), const2((128, 128)),
                  const2((1, 128)), const3((4, 128, 256)), const2((TQ, TK))],
        out_specs=pl.BlockSpec((TQ, 256), lambda b, i: (b * nq + i, 0)),
        out_shape=jax.ShapeDtypeStruct((batch * seq, 256), BF16),
        scratch_shapes=[pltpu.VMEM((seq, 256), BF16), pltpu.VMEM((seq, 128), BF16),
                        pltpu.VMEM((nc, TQ, TK), F32)],
        compiler_params=_attn_params(),
        name="dsa_attn",
    )(bm, bm, cidx, cidx, d, head_mask, idx_mask, k["rk"], k["pvs"], k["one_row"], k["pt4"], k["tril"])


def _mix_kernel(x_ref, g1_ref, wg_ref, o0_ref, o1_ref, o2_ref, o3_ref, wb_ref, wo_ref, y_ref):
    x = x_ref[...]
    hb = _rms(x, g1_ref[...]).astype(BF16)
    mixed = jnp.zeros((TM, D_MODEL), F32)
    for n, o_ref in enumerate((o0_ref, o1_ref, o2_ref, o3_ref)):
        zg = jnp.dot(hb, wg_ref[:, n * D_MODEL:(n + 1) * D_MODEL], preferred_element_type=F32)
        gate = 1.0 / (1.0 + jnp.exp(-zg))
        lifted = jnp.dot(o_ref[...], wb_ref[n], preferred_element_type=F32)
        mixed = mixed + gate * lifted
    y_ref[...] = x + jnp.dot(mixed.astype(BF16), wo_ref[...], preferred_element_type=F32)


def _mix(x2, g1, wg, outs, wb, wo):
    t = x2.shape[0]
    rows = lambda w: pl.BlockSpec((TM, w), lambda i: (i, 0))
    return pl.pallas_call(
        _mix_kernel,
        grid=(t // TM,),
        in_specs=[rows(D_MODEL), pl.BlockSpec((1, D_MODEL), lambda i: (0, 0)),
                  pl.BlockSpec((D_MODEL, 4 * D_MODEL), lambda i: (0, 0)),
                  rows(256), rows(256), rows(256), rows(256),
                  pl.BlockSpec((4, 256, D_MODEL), lambda i: (0, 0, 0)),
                  pl.BlockSpec((D_MODEL, D_MODEL), lambda i: (0, 0))],
        out_specs=rows(D_MODEL),
        out_shape=jax.ShapeDtypeStruct((t, D_MODEL), F32),
        compiler_params=_row_params(),
        name="gate_mix",
    )(x2, g1, wg, *outs, wb, wo)


def _ffn_kernel(final, x_ref, g2_ref, wu_ref, wd_ref, gf_ref, y_ref):
    x = x_ref[...]
    hb = _rms(x, g2_ref[...]).astype(BF16)
    acc = jnp.zeros((TM, D_MODEL), F32)
    for jc in range(D_FF // D_MODEL):
        u = jnp.dot(hb, wu_ref[:, jc * D_MODEL:(jc + 1) * D_MODEL], preferred_element_type=F32)
        u = jnp.square(jnp.maximum(u, 0.0)).astype(BF16)
        acc = acc + jnp.dot(u, wd_ref[jc * D_MODEL:(jc + 1) * D_MODEL, :], preferred_element_type=F32)
    y = x + acc
    if final:
        y = _rms(y, gf_ref[...])
    y_ref[...] = y


def _ffn(x2, g2, wu, wd, gf, final):
    t = x2.shape[0]
    rows = pl.BlockSpec((TM, D_MODEL), lambda i: (i, 0))
    vec = pl.BlockSpec((1, D_MODEL), lambda i: (0, 0))
    return pl.pallas_call(
        functools.partial(_ffn_kernel, final),
        grid=(t // TM,),
        in_specs=[rows, vec, pl.BlockSpec((D_MODEL, D_FF), lambda i: (0, 0)),
                  pl.BlockSpec((D_FF, D_MODEL), lambda i: (0, 0)), vec],
        out_specs=rows,
        out_shape=jax.ShapeDtypeStruct((t, D_MODEL), F32),
        compiler_params=_row_params(),
        name="ffn",
    )(x2, g2, wu, wd, gf)


def _compress_weights(cmp_w, cmp_pe):
    wk = cmp_w[0].reshape(NSA_CMP_LEN, HEAD_DIM, HEAD_DIM)
    wv = cmp_w[1].reshape(NSA_CMP_LEN, HEAD_DIM, HEAD_DIM)
    zero = jnp.zeros_like(wk)
    full = jnp.concatenate([jnp.concatenate([wk, zero], axis=2), jnp.concatenate([zero, wv], axis=2)], axis=1)
    half = NSA_CMP_LEN // 2
    w_top = full[:half].reshape(half * 128, 128).astype(BF16)
    w_bot = full[half:].reshape(half * 128, 128).astype(BF16)
    pe = jnp.concatenate([cmp_pe[0], cmp_pe[1]], axis=1)
    pe2 = pe.reshape(2, half * 128)
    return w_top, w_bot, pe2


def _overlap_matrix(seq):
    n_rows = seq // NSA_CMP_STRIDE
    cs = np.arange(n_rows) * NSA_CMP_STRIDE
    sb = np.arange(LANES) * NSA_SEL_LEN
    ov = np.maximum(np.minimum(cs[:, None] + NSA_CMP_LEN, sb[None, :] + NSA_SEL_LEN)
                    - np.maximum(cs[:, None], sb[None, :]), 0).astype(np.float32) / NSA_CMP_LEN
    ov[:, seq // NSA_SEL_LEN:] = 0.0
    ov[(seq - NSA_CMP_LEN) // NSA_CMP_STRIDE + 1:, :] = 0.0
    return jnp.asarray(ov)


def kernel(x, positions, norm1_g, w_in, mla_q_norm_g, mla_w_uq, mla_kv_norm_g, mla_w_ukv, nsa_cmp_pe, nsa_cmp_w,
           fox_f_bias, w_branch, w_out, norm2_g, w_up, w_down, final_g):
    batch, seq, _ = x.shape
    depth = w_in.shape[0]
    assert seq % TQ == 0 and seq % TM == 0 and seq // NSA_SEL_LEN <= LANES and TQ == TK
    assert seq // NSA_CMP_STRIDE <= LANES and (seq // NSA_CMP_STRIDE) % 8 == 0

    src, scale = _inproj_layout()
    q_src, k_src, v_src = _mla_up_layouts()
    tab = _rope_tables(positions)
    head_mask = _head_masks(N_HEADS, 256)
    idx_mask = _head_masks(IDX_HEADS, 256)
    consts = _placement_constants(seq)
    eye = jnp.eye(128, dtype=F32)
    overlap = _overlap_matrix(seq)

    x2 = x.reshape(batch * seq, D_MODEL)
    for l in range(depth):
        w_packed = _pack_cols(w_in[l, :, :GATE_BASE], src, scale).astype(BF16)
        wg = w_in[l, :, GATE_BASE:].astype(BF16)
        wuq = _pack_cols(mla_w_uq[l], q_src, np.full(512, (MLA_NOPE + MLA_ROPE) ** -0.5, np.float32)).astype(BF16)
        wuk = _pack_cols(mla_w_ukv[l], k_src, np.ones(512, np.float32)).astype(BF16)
        wuv = jnp.take(mla_w_ukv[l], jnp.asarray(v_src), axis=1).astype(BF16)
        fb = jnp.zeros((1, 128), F32).at[0, :N_HEADS].set(fox_f_bias[l])
        g1 = norm1_g[l][None, :]

        a, bm, kvc, cidx, d, e = _inproj(x2, g1, w_packed, tab, mla_q_norm_g[l][None, :], wuq,
                                          mla_kv_norm_g[l][None, :], wuk, wuv, fb, seq)
        w_top, w_bot, pe2 = _compress_weights(nsa_cmp_w[l], nsa_cmp_pe[l])
        kvcmp = _compress(kvc.reshape(batch, seq // NSA_CMP_STRIDE, NSA_CMP_STRIDE * 128), pe2, w_top, w_bot)

        o_mla = _mla(a, consts, batch, seq)
        o_nsa = _nsa(bm, kvcmp, d, head_mask, overlap, consts, batch, seq)
        o_fox = _fox(e, d, head_mask, eye, consts, batch, seq)
        o_dsa = _dsa(bm, cidx, d, head_mask, idx_mask, consts, batch, seq)

        x2 = _mix(x2, g1, wg, (o_mla, o_nsa, o_fox, o_dsa), w_branch[l].astype(BF16), w_out[l].astype(BF16))
        x2 = _ffn(x2, norm2_g[l][None, :], w_up[l].astype(BF16), w_down[l].astype(BF16), final_g[None, :],
                  l == depth - 1)
    return x2.reshape(batch, seq, D_MODEL)
```

```python
import functools

import numpy as np
import jax
import jax.numpy as jnp
from jax import lax
from jax.experimental import pallas as pl
from jax.experimental.pallas import tpu as pltpu

F32 = jnp.float32
BF16 = jnp.bfloat16

D_MODEL = 1024
ROPE_THETA = 500000.0
NEG_INF = -1e30
NORM_EPS = 1e-6
HEAD_DIM = 64
N_HEADS = 4
D_FF = 4 * D_MODEL
MLA_NOPE, MLA_ROPE, MLA_V = 64, 32, 64
MLA_Q_LORA, MLA_KV_LORA = 256, 128
NSA_CMP_LEN, NSA_CMP_STRIDE, NSA_SEL_LEN = 32, 16, 64
NSA_TOP_N, NSA_WINDOW, NSA_FORCED_SCORE = 8, 512, 1e4
IDX_HEADS, IDX_DIM = 8, 32
DSA_TOP_K, DSA_KEEP_DIV = 256, 4
MLA_COLS, NSA_COLS, FOX_COLS, DSA_COLS = 416, 652, 772, 680
GATE_BASE = MLA_COLS + NSA_COLS + FOX_COLS + DSA_COLS

LANES = 128
TQ = 256
TK = 256
TM = 256
VMEM_LIMIT = 56 * 1024 * 1024
INT_MIN = -2 ** 31

P_CQ, P_CKV, P_KR = 0, 256, 384
P_NQ, P_DQ = 512, 768
P_KVC, P_KVS, P_KVW, P_KVD = 1024, 1152, 1280, 1408
P_QI, P_KI = 1536, 1792
P_G, P_W = 2048, 2176
P_FQ, P_FK, P_FV, P_FF = 2304, 2560, 2816, 3072
P_TOTAL = 3200


def _inproj_layout():
    src = -np.ones(P_TOTAL, np.int64)
    scale = np.ones(P_TOTAL, np.float32)
    mla, nsa, fox, dsa = 0, MLA_COLS, MLA_COLS + NSA_COLS, MLA_COLS + NSA_COLS + FOX_COLS
    ar = np.arange
    src[P_CQ:P_CQ + 256] = mla + ar(256)
    src[P_CKV:P_CKV + 128] = mla + 256 + ar(128)
    src[P_KR + 64:P_KR + 96] = mla + 384 + ar(32)
    src[P_NQ:P_NQ + 256] = nsa + ar(256)
    scale[P_NQ:P_NQ + 256] = HEAD_DIM ** -0.5
    src[P_DQ:P_DQ + 256] = dsa + ar(256)
    scale[P_DQ:P_DQ + 256] = HEAD_DIM ** -0.5
    for base, k_off, v_off, origin in ((P_KVC, 256, 320, nsa), (P_KVS, 384, 448, nsa),
                                       (P_KVW, 512, 576, nsa), (P_KVD, 256, 320, dsa)):
        src[base:base + 64] = origin + k_off + ar(64)
        src[base + 64:base + 128] = origin + v_off + ar(64)
    src[P_QI:P_QI + 256] = dsa + 384 + ar(256)
    scale[P_QI:P_QI + 256] = (IDX_DIM * IDX_HEADS) ** -0.5
    for r in range(IDX_HEADS):
        src[P_KI + 32 * r:P_KI + 32 * r + 32] = dsa + 640 + ar(32)
    src[P_G:P_G + 12] = nsa + 640 + ar(12)
    src[P_W:P_W + 8] = dsa + 672 + ar(8)
    src[P_FQ:P_FQ + 256] = fox + ar(256)
    scale[P_FQ:P_FQ + 256] = HEAD_DIM ** -0.5
    src[P_FK:P_FK + 256] = fox + 256 + ar(256)
    src[P_FV:P_FV + 256] = fox + 512 + ar(256)
    src[P_FF:P_FF + 4] = fox + 768 + ar(4)
    return src, scale


def _pack_cols(w, src, scale):
    cols = jnp.take(w, jnp.asarray(np.maximum(src, 0)), axis=1)
    return cols * jnp.asarray(np.where(src >= 0, scale, 0.0).astype(np.float32))[None, :]


def _mla_up_layouts():
    q_src = -np.ones(512, np.int64)
    k_src = -np.ones(512, np.int64)
    v_src = np.zeros(256, np.int64)
    for h in range(N_HEADS):
        q_src[h * 128:h * 128 + 96] = h * 96 + np.arange(96)
        k_src[h * 128:h * 128 + 64] = h * 128 + np.arange(64)
        v_src[h * 64:h * 64 + 64] = h * 128 + 64 + np.arange(64)
    return q_src, k_src, v_src


def _rope_tables(positions):
    pos = positions.astype(F32).reshape(-1)[:, None]
    n = pos.shape[0]
    parts, offs = [], {}
    col = 0
    for name, rot in (("mla", MLA_ROPE), ("head", HEAD_DIM // 4), ("idx", IDX_DIM // 4)):
        inv = ROPE_THETA ** (-jnp.arange(0, rot, 2, dtype=F32) / rot)
        ang = pos * inv[None, :]
        parts += [jnp.cos(ang), jnp.sin(ang)]
        offs[name] = (col, col + rot // 2, rot // 2)
        col += rot
    one_col = col
    parts.append(jnp.ones((n, 64 - col), F32))
    cs = jnp.concatenate(parts, axis=1)

    place = np.zeros((64, 768), np.float32)

    def fill(cos_base, sin_base, name, group, first, n_groups):
        c0, s0, half = offs[name]
        place[one_col, cos_base:cos_base + 128] = 1.0
        for g in range(n_groups):
            lo = g * group + first
            for j in range(half):
                for lane, sign in ((lo + j, -1.0), (lo + half + j, 1.0)):
                    place[one_col, cos_base + lane] = 0.0
                    place[c0 + j, cos_base + lane] = 1.0
                    place[s0 + j, sin_base + lane] = sign

    fill(0, 128, "mla", 128, 64, 1)
    fill(256, 384, "head", 64, 0, 2)
    fill(512, 640, "idx", 32, 0, 4)
    return jnp.dot(cs, jnp.asarray(place), precision=lax.Precision.HIGHEST)


def _head_masks(n_heads, width):
    lane = np.arange(width)
    return jnp.asarray((lane[None, :] // (width // n_heads) == np.arange(n_heads)[:, None]), BF16)


def _placement_constants(seq):
    d = np.arange(64)
    rk = np.zeros((128, 256), np.float32)
    pvs = np.zeros((128, 128), np.float32)
    pv4 = np.zeros((4, 256, 128), np.float32)
    pt4 = np.zeros((4, 128, 256), np.float32)
    pvs[64 + d, d] = 1.0
    for h in range(N_HEADS):
        rk[d, h * 64 + d] = 1.0
        pv4[h, h * 64 + d, d] = 1.0
        pt4[h, d, h * 64 + d] = 1.0
    one_row = np.zeros((1, 128), np.float32)
    one_row[0, 64] = 1.0
    r, c = np.arange(TQ)[:, None], np.arange(TK)[None, :]
    tril = np.where(c <= r, 0.0, NEG_INF).astype(np.float32)
    triu = np.where(c > r, 0.0, NEG_INF).astype(np.float32)
    nc = seq // TK
    key_blk = (np.arange(seq) // NSA_SEL_LEN).reshape(nc, 1, TK)
    expand = (key_blk == np.arange(LANES).reshape(1, LANES, 1)).astype(np.float32)
    bf = lambda a: jnp.asarray(a, BF16)
    return dict(rk=bf(rk), pvs=bf(pvs), pv4=bf(pv4), pt4=bf(pt4), one_row=jnp.asarray(one_row),
                tril=jnp.asarray(tril), triu=jnp.asarray(triu), expand=bf(expand),
                eye=bf(np.eye(256, dtype=np.float32)), eyef=jnp.asarray(np.eye(128, dtype=np.float32)),
                trilt=jnp.asarray(np.ascontiguousarray(tril.T)))


def _rms(x, g):
    return x * lax.rsqrt(jnp.mean(x * x, axis=-1, keepdims=True) + NORM_EPS) * g


def _rope_tile(x, cos, sin, is_x1, half):
    fwd = pltpu.roll(x, LANES - half, axis=1)
    bwd = pltpu.roll(x, half, axis=1)
    return x * cos + jnp.where(is_x1, fwd, bwd) * sin


def _nt_dot(a, b):
    return lax.dot_general(a, b, (((1,), (1,)), ((), ())), preferred_element_type=F32)


def _attn_params():
    return pltpu.CompilerParams(dimension_semantics=("arbitrary", "arbitrary"), vmem_limit_bytes=VMEM_LIMIT)


def _row_params():
    return pltpu.CompilerParams(dimension_semantics=("arbitrary",), vmem_limit_bytes=VMEM_LIMIT)


def _inproj_kernel(tiles_per_seq, x_ref, g1_ref, w_ref, tab_ref, gq_ref, wuq_ref, gkv_ref, wuk_ref,
                   wuv_ref, fb_ref, a_ref, b_ref, kvc_ref, c_ref, d_ref, e_ref, carry_ref):
    i = pl.program_id(0)
    x = x_ref[...]
    hb = _rms(x, g1_ref[...]).astype(BF16)
    z = jnp.dot(hb, w_ref[...], preferred_element_type=F32)

    lane = lax.broadcasted_iota(jnp.int32, (TM, LANES), 1)
    cos1, sin1 = tab_ref[:, 0:128], tab_ref[:, 128:256]
    cos3, sin3 = tab_ref[:, 256:384], tab_ref[:, 384:512]
    cos5, sin5 = tab_ref[:, 512:640], tab_ref[:, 640:768]
    x1_mla = (lane >= 64) & (lane < 80)
    x1_head = (lane & 63) < 8
    x1_idx = (lane & 31) < 4
    k_half = lane < 64
    cos4 = jnp.where(k_half, cos3, 1.0)
    sin4 = jnp.where(k_half, sin3, 0.0)

    cqn = _rms(z[:, P_CQ:P_CQ + 256], gq_ref[...]).astype(BF16)
    q_up = jnp.dot(cqn, wuq_ref[...], preferred_element_type=F32)
    ckvn = _rms(z[:, P_CKV:P_CKV + 128], gkv_ref[...]).astype(BF16)
    k_up = jnp.dot(ckvn, wuk_ref[...], preferred_element_type=F32)
    v_up = jnp.dot(ckvn, wuv_ref[...], preferred_element_type=F32)
    k_rot = _rope_tile(z[:, P_KR:P_KR + 128], cos1, sin1, x1_mla, 16)
    for h in range(N_HEADS):
        sl = slice(h * 128, (h + 1) * 128)
        a_ref[:, sl] = _rope_tile(q_up[:, sl], cos1, sin1, x1_mla, 16).astype(BF16)
        a_ref[:, 512 + h * 128:512 + (h + 1) * 128] = (k_up[:, sl] + k_rot).astype(BF16)
    a_ref[:, 1024:1280] = v_up.astype(BF16)

    for t in range(4):
        b_ref[:, t * 128:(t + 1) * 128] = _rope_tile(
            z[:, P_NQ + t * 128:P_NQ + (t + 1) * 128], cos3, sin3, x1_head, 8).astype(BF16)
    kvc_ref[...] = _rope_tile(z[:, P_KVC:P_KVC + 128], cos4, sin4, x1_head & k_half, 8).astype(BF16)
    for t, off in enumerate((P_KVS, P_KVW, P_KVD)):
        b_ref[:, 512 + t * 128:512 + (t + 1) * 128] = _rope_tile(
            z[:, off:off + 128], cos4, sin4, x1_head & k_half, 8).astype(BF16)

    for t in range(4):
        c_ref[:, t * 128:(t + 1) * 128] = _rope_tile(
            z[:, P_QI + t * 128:P_QI + (t + 1) * 128], cos5, sin5, x1_idx, 4).astype(BF16)

    d_ref[:, 0:128] = 1.0 / (1.0 + jnp.exp(-z[:, P_G:P_G + 128]))
    d_ref[:, 128:256] = z[:, P_W:P_W + 128]
    f = z[:, P_FF:P_FF + 128] + fb_ref[...]
    log_f = jnp.minimum(f, 0.0) - jnp.log(1.0 + jnp.exp(-jnp.abs(f)))

    @pl.when(i % tiles_per_seq == 0)
    def _():
        carry_ref[...] = jnp.zeros_like(carry_ref)

    row = lax.broadcasted_iota(jnp.int32, (TM, TM), 0)
    col = lax.broadcasted_iota(jnp.int32, (TM, TM), 1)
    tri = jnp.where(col <= row, 1.0, 0.0).astype(F32)
    c = jnp.dot(tri, log_f, preferred_element_type=F32, precision=lax.Precision.HIGHEST)
    c = c + carry_ref[0:1, :]
    d_ref[:, 256:384] = c
    carry_ref[...] = jnp.broadcast_to(c[TM - 1:TM, :], carry_ref.shape)

    e_ref[...] = z[:, P_FQ:P_FQ + 768].astype(BF16)


def _inproj(x2, g1, w_packed, tab, gq, wuq, gkv, wuk, wuv, fb, seq):
    t = x2.shape[0]
    full = lambda shape: pl.BlockSpec(shape, lambda i: (0, 0))
    rows = lambda w: pl.BlockSpec((TM, w), lambda i: (i, 0))
    return pl.pallas_call(
        functools.partial(_inproj_kernel, seq // TM),
        grid=(t // TM,),
        in_specs=[rows(D_MODEL), full((1, D_MODEL)), full((D_MODEL, P_TOTAL)), rows(768),
                  full((1, 256)), full((256, 512)), full((1, 128)), full((128, 512)), full((128, 256)),
                  full((1, 128))],
        out_specs=[rows(1280), rows(896), rows(128), rows(512), rows(384), rows(768)],
        out_shape=[jax.ShapeDtypeStruct((t, 1280), BF16), jax.ShapeDtypeStruct((t, 896), BF16),
                   jax.ShapeDtypeStruct((t, 128), BF16), jax.ShapeDtypeStruct((t, 512), BF16),
                   jax.ShapeDtypeStruct((t, 384), F32), jax.ShapeDtypeStruct((t, 768), BF16)],
        scratch_shapes=[pltpu.VMEM((8, 128), F32)],
        compiler_params=_row_params(),
        name="inproj",
    )(x2, g1, w_packed, tab, gq, wuq, gkv, wuk, wuv, fb)


def _compress_kernel(r_ref, pe_ref, wt_ref, wb_ref, o_ref):
    r = r_ref[0].astype(F32)
    top = jnp.dot((r + pe_ref[0:1, :]).astype(BF16), wt_ref[...], preferred_element_type=F32)
    bot = jnp.dot((r + pe_ref[1:2, :]).astype(BF16), wb_ref[...], preferred_element_type=F32)
    n = bot.shape[0]
    o_ref[0] = (top + pltpu.roll(bot, n - 1, axis=0)).astype(BF16)


def _compress(kvc3, pe2, w_top, w_bot):
    b, n, _ = kvc3.shape
    return pl.pallas_call(
        _compress_kernel,
        grid=(b,),
        in_specs=[pl.BlockSpec((1, n, 2048), lambda i: (i, 0, 0)), pl.BlockSpec((2, 2048), lambda i: (0, 0)),
                  pl.BlockSpec((2048, 128), lambda i: (0, 0)), pl.BlockSpec((2048, 128), lambda i: (0, 0))],
        out_specs=pl.BlockSpec((1, n, 128), lambda i: (i, 0, 0)),
        out_shape=jax.ShapeDtypeStruct((b, n, 128), BF16),
        compiler_params=_row_params(),
        name="nsa_compress",
    )(kvc3, pe2, w_top, w_bot)


def _flash_init(rows):
    return jnp.full((rows, 1), NEG_INF, F32), jnp.zeros((rows, LANES), F32)


def _flash_update(carry, s, v):
    m, acc = carry
    m_new = jnp.maximum(m, jnp.max(s, axis=-1, keepdims=True))
    p = jnp.exp(s - m_new).astype(BF16)
    alpha = jnp.exp(m - m_new)
    return m_new, alpha * acc + jnp.dot(p, v, preferred_element_type=F32)


def _flash_update_many(carries, scores, values):
    ms = [jnp.maximum(m, jnp.max(s, axis=-1, keepdims=True)) for (m, _), s in zip(carries, scores)]
    ps = [jnp.exp(s - m).astype(BF16) for s, m in zip(scores, ms)]
    alphas = [jnp.exp(m_old - m) for (m_old, _), m in zip(carries, ms)]
    pvs = [jnp.dot(p, v, preferred_element_type=F32) for p, v in zip(ps, values)]
    return tuple((m, a * acc + pv) for m, a, (_, acc), pv in zip(ms, alphas, carries, pvs))


def _chunk_loop(n, step, init):
    carry = lax.fori_loop(0, n // 2, lambda c2, cr: step(2 * c2 + 1, step(2 * c2, cr)), init)
    return lax.fori_loop(2 * (n // 2), n, step, carry)


def _flash_out(acc):
    return acc / jnp.maximum(acc[:, 64:65], 1e-30)


def _place_heads(outs, pt_ref):
    res = None
    for h, o in enumerate(outs):
        t = jnp.dot(o.astype(BF16), pt_ref[h], preferred_element_type=F32)
        res = t if res is None else res + t
    return res.astype(BF16)


def _chunk_rows(c):
    return pl.ds(pl.multiple_of(c * TK, TK), TK)


def _stack_heads(x, mask_ref, n):
    return jnp.concatenate([x * mask_ref[h:h + 1, :] for h in range(n)], axis=0)


def _mla_kernel(q_ref, k_ref, v_ref, pv_ref, one_ref, pt_ref, tril_ref, o_ref, vaug_ref):
    i = pl.program_id(1)

    @pl.when(i == 0)
    def _():
        v = v_ref[...]
        for h in range(N_HEADS):
            vaug_ref[h] = (jnp.dot(v, pv_ref[h], preferred_element_type=F32) + one_ref[...]).astype(BF16)

    qs = [q_ref[:, h * 128:(h + 1) * 128] for h in range(N_HEADS)]

    def step(c, carries, bias=None):
        rows = _chunk_rows(c)
        scores = [_nt_dot(qs[h], k_ref[rows, h * 128:(h + 1) * 128]) for h in range(N_HEADS)]
        if bias is not None:
            scores = [s + bias for s in scores]
        return _flash_update_many(carries, scores, [vaug_ref[h, rows, :] for h in range(N_HEADS)])

    carries = _chunk_loop(i, step, tuple(_flash_init(TQ) for _ in range(N_HEADS)))
    carries = step(i, carries, tril_ref[...])
    o_ref[...] = _place_heads([_flash_out(acc) for _, acc in carries], pt_ref)


def _mla(a, k, batch, seq):
    nq = seq // TQ
    const2 = lambda shape: pl.BlockSpec(shape, lambda b, i: (0, 0))
    const3 = lambda shape: pl.BlockSpec(shape, lambda b, i: (0, 0, 0))
    return pl.pallas_call(
        _mla_kernel,
        grid=(batch, nq),
        in_specs=[pl.BlockSpec((TQ, 512), lambda b, i: (b * nq + i, 0)),
                  pl.BlockSpec((seq, 512), lambda b, i: (b, 1)),
                  pl.BlockSpec((seq, 256), lambda b, i: (b, 4)),
                  const3((4, 256, 128)), const2((1, 128)), const3((4, 128, 256)), const2((TQ, TK))],
        out_specs=pl.BlockSpec((TQ, 256), lambda b, i: (b * nq + i, 0)),
        out_shape=jax.ShapeDtypeStruct((batch * seq, 256), BF16),
        scratch_shapes=[pltpu.VMEM((N_HEADS, seq, 128), BF16)],
        compiler_params=_attn_params(),
        name="mla_attn",
    )(a, a, a, k["pv4"], k["one_row"], k["pt4"], k["tril"])


def _fox_kernel(n_chunks, q_ref, k_ref, v_ref, cq_ref, call_ref, hm_ref, eye_ref, pv_ref, one_ref, pt_ref,
                tril_ref, o_ref, vaug_ref, ct_ref):
    i = pl.program_id(1)

    @pl.when(i == 0)
    def _():
        v = v_ref[...]
        for h in range(N_HEADS):
            vaug_ref[h] = (jnp.dot(v, pv_ref[h], preferred_element_type=F32) + one_ref[...]).astype(BF16)
        ct = lax.dot_general(eye_ref[...], call_ref[...], (((1,), (1,)), ((), ())),
                             preferred_element_type=F32, precision=lax.Precision.HIGHEST)
        for c in range(n_chunks):
            ct_ref[c] = ct[0:8, c * TK:(c + 1) * TK]

    q_all = q_ref[...]
    cq = cq_ref[...]
    qs = [q_all * hm_ref[h:h + 1, :] for h in range(N_HEADS)]
    cts = [cq[:, h:h + 1] for h in range(N_HEADS)]

    def step(c, carries, bias=None):
        rows = _chunk_rows(c)
        k = k_ref[rows, :]
        cs = ct_ref[c]
        scores = [_nt_dot(qs[h], k) + (cts[h] - cs[h:h + 1, :]) for h in range(N_HEADS)]
        if bias is not None:
            scores = [s + bias for s in scores]
        return _flash_update_many(carries, scores, [vaug_ref[h, rows, :] for h in range(N_HEADS)])

    carries = _chunk_loop(i, step, tuple(_flash_init(TQ) for _ in range(N_HEADS)))
    carries = step(i, carries, tril_ref[...])
    o_ref[...] = _place_heads([_flash_out(acc) for _, acc in carries], pt_ref)


def _fox(e, d, head_mask, eye, k, batch, seq):
    nq = seq // TQ
    nc = seq // TK
    const2 = lambda shape: pl.BlockSpec(shape, lambda b, i: (0, 0))
    const3 = lambda shape: pl.BlockSpec(shape, lambda b, i: (0, 0, 0))
    return pl.pallas_call(
        functools.partial(_fox_kernel, nc),
        grid=(batch, nq),
        in_specs=[pl.BlockSpec((TQ, 256), lambda b, i: (b * nq + i, 0)),
                  pl.BlockSpec((seq, 256), lambda b, i: (b, 1)),
                  pl.BlockSpec((seq, 256), lambda b, i: (b, 2)),
                  pl.BlockSpec((TQ, 128), lambda b, i: (b * nq + i, 2)),
                  pl.BlockSpec((seq, 128), lambda b, i: (b, 2)),
                  const2((N_HEADS, 256)), const2((128, 128)),
                  const3((4, 256, 128)), const2((1, 128)), const3((4, 128, 256)), const2((TQ, TK))],
        out_specs=pl.BlockSpec((TQ, 256), lambda b, i: (b * nq + i, 0)),
        out_shape=jax.ShapeDtypeStruct((batch * seq, 256), BF16),
        scratch_shapes=[pltpu.VMEM((N_HEADS, seq, 128), BF16), pltpu.VMEM((nc, 8, TK), F32)],
        compiler_params=_attn_params(),
        name="fox_attn",
    )(e, e, e, d, d, head_mask, eye, k["pv4"], k["one_row"], k["pt4"], k["tril"])


def _nsa_kernel(n_sel_blocks, q_ref, kvs_ref, kvw_ref, kvc_ref, g_ref, hm_ref, rk_ref, pvs_ref, one_ref,
                pt_ref, tril_ref, triu_ref, ex_ref, ov_ref, o_ref, ks_ref, vs_ref, kw_ref, vw_ref, kc_ref,
                vc_ref):
    i = pl.program_id(1)

    @pl.when(i == 0)
    def _():
        rk, pvs, one = rk_ref[...], pvs_ref[...], one_ref[...]
        for src_ref, k_dst, v_dst in ((kvs_ref, ks_ref, vs_ref), (kvw_ref, kw_ref, vw_ref)):
            kv = src_ref[...]
            k_dst[...] = jnp.dot(kv, rk, preferred_element_type=F32).astype(BF16)
            v_dst[...] = (jnp.dot(kv, pvs, preferred_element_type=F32) + one).astype(BF16)
        kv = kvc_ref[0]
        kc_ref[...] = jnp.dot(kv, rk, preferred_element_type=F32).astype(BF16)
        vc_ref[...] = jnp.dot(kv, pvs, preferred_element_type=F32).astype(BF16)

    g = g_ref[...]
    q_stack = _stack_heads(q_ref[...], hm_ref, N_HEADS)
    n_cmp_rows = kc_ref.shape[0]
    head = lambda x, h: x[h * TQ:(h + 1) * TQ]

    t_c = i * TQ + lax.broadcasted_iota(jnp.int32, (TQ, n_cmp_rows), 0)
    c_idx = lax.broadcasted_iota(jnp.int32, (TQ, n_cmp_rows), 1)
    cmp_mask = c_idx * NSA_CMP_STRIDE + (NSA_CMP_LEN - 1) <= t_c
    s_cmp = _nt_dot(q_stack, kc_ref[...])
    p_sum = jnp.zeros((TQ, n_cmp_rows), F32)
    o_cmp = []
    for h in range(N_HEADS):
        s = jnp.where(cmp_mask, head(s_cmp, h), NEG_INF)
        m = jnp.max(s, axis=-1, keepdims=True)
        p = jnp.where(cmp_mask, jnp.exp(s - m), 0.0)
        p = p / jnp.maximum(jnp.sum(p, axis=-1, keepdims=True), 1e-30)
        p_sum = p_sum + p
        o_cmp.append(jnp.dot(p.astype(BF16), vc_ref[...], preferred_element_type=F32))

    imp = jnp.dot(p_sum, ov_ref[...], preferred_element_type=F32, precision=lax.Precision.HIGHEST)
    t_r = i * TQ + lax.broadcasted_iota(jnp.int32, (TQ, LANES), 0)
    j = lax.broadcasted_iota(jnp.int32, (TQ, LANES), 1)
    t_blk = t_r >> 6
    forced = (j == 0) | (j == t_blk) | (j == t_blk - 1)
    imp = jnp.where(forced, NSA_FORCED_SCORE, imp)
    imp = jnp.where(j * NSA_SEL_LEN <= t_r, imp, NEG_INF)
    imp = jnp.where(j < n_sel_blocks, imp, -3e38)
    rank = jnp.zeros((TQ, LANES), F32)
    for i2 in range(n_sel_blocks):
        col = imp[:, i2:i2 + 1]
        beats = jnp.where(col > imp, 1.0, jnp.where(col == imp, jnp.where(j > i2, 1.0, 0.0), 0.0))
        rank = rank + beats
    member = jnp.where(rank < float(NSA_TOP_N), jnp.where(j < n_sel_blocks, 1.0, 0.0), 0.0).astype(BF16)

    def sel_step(c, carry, bias=None):
        rows = _chunk_rows(c)
        in_sel = jnp.dot(member, ex_ref[c], preferred_element_type=F32)
        pen = jnp.where(in_sel > 0.5, 0.0, NEG_INF)
        if bias is not None:
            pen = pen + bias
        s_all = _nt_dot(q_stack, ks_ref[rows, :])
        s = jnp.concatenate([head(s_all, h) + pen for h in range(N_HEADS)], axis=0)
        return _flash_update(carry, s, vs_ref[rows, :])

    sel = _chunk_loop(i, sel_step, _flash_init(N_HEADS * TQ))
    _, sel_acc = sel_step(i, sel, tril_ref[...])

    win = _flash_init(N_HEADS * TQ)
    for back, bias_ref in ((2, triu_ref), (1, None), (0, tril_ref)):
        c = jnp.maximum(i - back, 0)
        rows = _chunk_rows(c)
        void = jnp.where(i >= back, 0.0, NEG_INF)
        pen = void if bias_ref is None else bias_ref[...] + void
        s_all = _nt_dot(q_stack, kw_ref[rows, :])
        s = jnp.concatenate([head(s_all, h) + pen for h in range(N_HEADS)], axis=0)
        win = _flash_update(win, s, vw_ref[rows, :])
    _, win_acc = win

    o_sel = _flash_out(sel_acc)
    o_win = _flash_out(win_acc)
    outs = []
    for h in range(N_HEADS):
        outs.append(g[:, h:h + 1] * o_cmp[h] + g[:, 4 + h:5 + h] * head(o_sel, h)
                    + g[:, 8 + h:9 + h] * head(o_win, h))
    o_ref[...] = _place_heads(outs, pt_ref)


def _nsa(bm, kvcmp, d, head_mask, overlap, k, batch, seq):
    nq = seq // TQ
    nc = seq // TK
    n_cmp_rows = kvcmp.shape[1]
    const2 = lambda shape: pl.BlockSpec(shape, lambda b, i: (0, 0))
    const3 = lambda shape: pl.BlockSpec(shape, lambda b, i: (0, 0, 0))
    return pl.pallas_call(
        functools.partial(_nsa_kernel, seq // NSA_SEL_LEN),
        grid=(batch, nq),
        in_specs=[pl.BlockSpec((TQ, 256), lambda b, i: (b * nq + i, 0)),
                  pl.BlockSpec((seq, 128), lambda b, i: (b, 4)),
                  pl.BlockSpec((seq, 128), lambda b, i: (b, 5)),
                  pl.BlockSpec((1, n_cmp_rows, 128), lambda b, i: (b, 0, 0)),
                  pl.BlockSpec((TQ, 128), lambda b, i: (b * nq + i, 0)),
                  const2((N_HEADS, 256)), const2((128, 256)), const2((128, 128)), const2((1, 128)),
                  const3((4, 128, 256)), const2((TQ, TK)), const2((TQ, TK)), const3((nc, LANES, TK)),
                  const2((n_cmp_rows, 128))],
        out_specs=pl.BlockSpec((TQ, 256), lambda b, i: (b * nq + i, 0)),
        out_shape=jax.ShapeDtypeStruct((batch * seq, 256), BF16),
        scratch_shapes=[pltpu.VMEM((seq, 256), BF16), pltpu.VMEM((seq, 128), BF16),
                        pltpu.VMEM((seq, 256), BF16), pltpu.VMEM((seq, 128), BF16),
                        pltpu.VMEM((n_cmp_rows, 256), BF16), pltpu.VMEM((n_cmp_rows, 128), BF16)],
        compiler_params=_attn_params(),
        name="nsa_attn",
    )(bm, bm, bm, kvcmp, d, head_mask, k["rk"], k["pvs"], k["one_row"], k["pt4"], k["tril"], k["triu"],
      k["expand"], overlap)


def _dsa_kernel(n_keep, idx_bits, q_ref, kvd_ref, qi_ref, ki_ref, w_ref, hm_ref, rk_ref, pvs_ref,
                one_ref, pt_ref, tril_ref, trilt_ref, eye_ref, eyef_ref, o_ref, kd_ref, vd_ref, sc_ref, qit_ref):
    i = pl.program_id(1)

    @pl.when(i == 0)
    def _():
        kv = kvd_ref[...]
        kd_ref[...] = jnp.dot(kv, rk_ref[...], preferred_element_type=F32).astype(BF16)
        vd_ref[...] = (jnp.dot(kv, pvs_ref[...], preferred_element_type=F32) + one_ref[...]).astype(BF16)

    head = lambda x, h: x[h * TQ:(h + 1) * TQ]

    qi_t = _nt_dot(eye_ref[...], qi_ref[...])
    feat = lax.broadcasted_iota(jnp.int32, qi_t.shape, 0)
    for h in range(IDX_HEADS):
        qit_ref[:, h * TQ:(h + 1) * TQ] = jnp.where((feat >> 5) == h, qi_t, 0.0).astype(BF16)
    w_t = lax.dot_general(eyef_ref[...], w_ref[...], (((1,), (1,)), ((), ())),
                          preferred_element_type=F32, precision=lax.Precision.HIGHEST)
    ws = [w_t[h:h + 1, :] for h in range(IDX_HEADS)]

    def score_chunk(c, bias):
        logits = jnp.dot(ki_ref[_chunk_rows(c), :], qit_ref[...], preferred_element_type=F32)
        score = jnp.maximum(logits[:, 0:TQ], 0.0) * ws[0]
        for h in range(1, IDX_HEADS):
            score = score + jnp.maximum(logits[:, h * TQ:(h + 1) * TQ], 0.0) * ws[h]
        score = jnp.where(score == 0.0, 0.0, score)
        if bias is not None:
            score = score + bias
        sc_ref[c] = score

    def score_body(c, carry):
        score_chunk(c, None)
        return carry

    lax.fori_loop(0, i, score_body, 0)
    score_chunk(i, trilt_ref[...])

    def count(pred):
        def body(c, part):
            hit = jnp.where(pred(sc_ref[c], c), 1.0, 0.0)
            return part + jnp.sum(hit.reshape(TK // 8, 8, TQ), axis=0)
        part = lax.fori_loop(0, i + 1, body, jnp.zeros((8, TQ), F32))
        return jnp.sum(part, axis=0, keepdims=True)

    def key_to_float(key):
        bits = jnp.where(key < 0, key ^ jnp.int32(0x7FFFFFFF), key)
        return lax.bitcast_convert_type(bits, F32)

    keep = float(n_keep)
    total = ((i + 1) * TK).astype(F32)
    cnt0 = count(lambda s, c: s >= 0.0)
    thr0 = jnp.where(cnt0 >= keep, jnp.int32(0), jnp.int32(INT_MIN))
    cnt0 = jnp.where(cnt0 >= keep, cnt0, total)

    def thr_bit(it, carry):
        thr, cnt_thr = carry
        cand = thr | jnp.left_shift(jnp.int32(1), 30 - it)
        cand_f = key_to_float(cand)
        cnt = count(lambda s, c: s >= cand_f)
        take = cnt >= keep
        return jnp.where(take, cand, thr), jnp.where(take, cnt, cnt_thr)

    thr, cnt_thr = lax.fori_loop(0, 31, thr_bit, (thr0, cnt0))
    thr_f = key_to_float(thr)

    def key_index(c):
        return c * TK + lax.broadcasted_iota(jnp.int32, (TK, TQ), 0)

    def resolve_ties(_):
        need = keep - count(lambda s, c: s > thr_f)

        def cut_bit(it, cut):
            cand = cut | jnp.left_shift(jnp.int32(1), idx_bits - 1 - it)
            cnt = count(lambda s, c: (s == thr_f) & (key_index(c) < cand))
            return jnp.where(cnt < need, cand, cut)

        return lax.fori_loop(0, idx_bits, cut_bit, jnp.zeros((1, TQ), jnp.int32))

    has_excess = jnp.max(jnp.abs(cnt_thr - keep)) > 0.0
    cut = lax.cond(has_excess, resolve_ties, lambda _: jnp.full((1, TQ), 2 ** idx_bits, jnp.int32), 0)

    q_stack = _stack_heads(q_ref[...], hm_ref, N_HEADS)
    eye = eye_ref[...]

    def attn_step(c, carry, bias=None):
        rows = _chunk_rows(c)
        sc = sc_ref[c]
        tie = jnp.where(key_index(c) <= cut, 1.0, 0.0)
        chosen_t = jnp.where(sc > thr_f, 1.0, jnp.where(sc == thr_f, tie, 0.0)).astype(BF16)
        chosen = _nt_dot(eye, chosen_t)
        pen = jnp.where(chosen > 0.5, 0.0, NEG_INF)
        if bias is not None:
            pen = pen + bias
        s_all = _nt_dot(q_stack, kd_ref[rows, :])
        s = jnp.concatenate([head(s_all, h) + pen for h in range(N_HEADS)], axis=0)
        return _flash_update(carry, s, vd_ref[rows, :])

    carry = _chunk_loop(i, attn_step, _flash_init(N_HEADS * TQ))
    _, acc = attn_step(i, carry, tril_ref[...])
    out = _flash_out(acc)
    o_ref[...] = _place_heads([head(out, h) for h in range(N_HEADS)], pt_ref)


def _dsa(bm, cidx, d, head_mask, k, batch, seq):
    nq = seq // TQ
    nc = seq // TK
    n_keep = min(DSA_TOP_K, seq // DSA_KEEP_DIV)
    assert n_keep <= TK
    idx_bits = int(np.log2(seq))
    assert 2 ** idx_bits == seq
    const2 = lambda shape: pl.BlockSpec(shape, lambda b, i: (0, 0))
    const3 = lambda shape: pl.BlockSpec(shape, lambda b, i: (0, 0, 0))
    return pl.pallas_call(
        functools.partial(_dsa_kernel, n_keep, idx_bits),
        grid=(batch, nq),
        in_specs=[pl.BlockSpec((TQ, 256), lambda b, i: (b * nq + i, 1)),
                  pl.BlockSpec((seq, 128), lambda b, i: (b, 6)),
                  pl.BlockSpec((TQ, 256), lambda b, i: (b * nq + i, 0)),
                  pl.BlockSpec((seq, 256), lambda b, i: (b, 1)),
                  pl.BlockSpec((TQ, 128), lambda b, i: (b * nq + i, 1)),
                  const2((N_HEADS, 256)), const2((128, 256)), const2((128, 128)),
                  const2((1, 128)), const3((4, 128, 256)), const2((TQ, TK)), const2((TK, TQ)),
                  const2((256, 256)), const2((128, 128))],
        out_specs=pl.BlockSpec((TQ, 256), lambda b, i: (b * nq + i, 0)),
        out_shape=jax.ShapeDtypeStruct((batch * seq, 256), BF16),
        scratch_shapes=[pltpu.VMEM((seq, 256), BF16), pltpu.VMEM((seq, 128), BF16),
                        pltpu.VMEM((nc, TK, TQ), F32), pltpu.VMEM((256, IDX_HEADS * TQ), BF16)],
        compiler_params=_attn_params(),
        name="dsa_attn",
    )(bm, bm, cidx, cidx, d, head_mask, k["rk"], k["pvs"], k["one_row"], k["pt4"], k["tril"], k["trilt"],
      k["eye"], k["eyef"])


def _mix_kernel(x_ref, g1_ref, wg_ref, o0_ref, o1_ref, o2_ref, o3_ref, wb_ref, wo_ref, y_ref):
    x = x_ref[...]
    hb = _rms(x, g1_ref[...]).astype(BF16)
    mixed = jnp.zeros((TM, D_MODEL), F32)
    for n, o_ref in enumerate((o0_ref, o1_ref, o2_ref, o3_ref)):
        zg = jnp.dot(hb, wg_ref[:, n * D_MODEL:(n + 1) * D_MODEL], preferred_element_type=F32)
        gate = 1.0 / (1.0 + jnp.exp(-zg))
        lifted = jnp.dot(o_ref[...], wb_ref[n], preferred_element_type=F32)
        mixed = mixed + gate * lifted
    y_ref[...] = x + jnp.dot(mixed.astype(BF16), wo_ref[...], preferred_element_type=F32)


def _mix(x2, g1, wg, outs, wb, wo):
    t = x2.shape[0]
    rows = lambda w: pl.BlockSpec((TM, w), lambda i: (i, 0))
    return pl.pallas_call(
        _mix_kernel,
        grid=(t // TM,),
        in_specs=[rows(D_MODEL), pl.BlockSpec((1, D_MODEL), lambda i: (0, 0)),
                  pl.BlockSpec((D_MODEL, 4 * D_MODEL), lambda i: (0, 0)),
                  rows(256), rows(256), rows(256), rows(256),
                  pl.BlockSpec((4, 256, D_MODEL), lambda i: (0, 0, 0)),
                  pl.BlockSpec((D_MODEL, D_MODEL), lambda i: (0, 0))],
        out_specs=rows(D_MODEL),
        out_shape=jax.ShapeDtypeStruct((t, D_MODEL), F32),
        compiler_params=_row_params(),
        name="gate_mix",
    )(x2, g1, wg, *outs, wb, wo)


def _ffn_kernel(final, x_ref, g2_ref, wu_ref, wd_ref, gf_ref, y_ref):
    x = x_ref[...]
    hb = _rms(x, g2_ref[...]).astype(BF16)
    acc = jnp.zeros((TM, D_MODEL), F32)
    for jc in range(D_FF // D_MODEL):
        u = jnp.dot(hb, wu_ref[:, jc * D_MODEL:(jc + 1) * D_MODEL], preferred_element_type=F32)
        u = jnp.square(jnp.maximum(u, 0.0)).astype(BF16)
        acc = acc + jnp.dot(u, wd_ref[jc * D_MODEL:(jc + 1) * D_MODEL, :], preferred_element_type=F32)
    y = x + acc
    if final:
        y = _rms(y, gf_ref[...])
    y_ref[...] = y


def _ffn(x2, g2, wu, wd, gf, final):
    t = x2.shape[0]
    rows = pl.BlockSpec((TM, D_MODEL), lambda i: (i, 0))
    vec = pl.BlockSpec((1, D_MODEL), lambda i: (0, 0))
    return pl.pallas_call(
        functools.partial(_ffn_kernel, final),
        grid=(t // TM,),
        in_specs=[rows, vec, pl.BlockSpec((D_MODEL, D_FF), lambda i: (0, 0)),
                  pl.BlockSpec((D_FF, D_MODEL), lambda i: (0, 0)), vec],
        out_specs=rows,
        out_shape=jax.ShapeDtypeStruct((t, D_MODEL), F32),
        compiler_params=_row_params(),
        name="ffn",
    )(x2, g2, wu, wd, gf)


def _compress_weights(cmp_w, cmp_pe):
    wk = cmp_w[0].reshape(NSA_CMP_LEN, HEAD_DIM, HEAD_DIM)
    wv = cmp_w[1].reshape(NSA_CMP_LEN, HEAD_DIM, HEAD_DIM)
    zero = jnp.zeros_like(wk)
    full = jnp.concatenate([jnp.concatenate([wk, zero], axis=2), jnp.concatenate([zero, wv], axis=2)], axis=1)
    half = NSA_CMP_LEN // 2
    w_top = full[:half].reshape(half * 128, 128).astype(BF16)
    w_bot = full[half:].reshape(half * 128, 128).astype(BF16)
    pe = jnp.concatenate([cmp_pe[0], cmp_pe[1]], axis=1)
    pe2 = pe.reshape(2, half * 128)
    return w_top, w_bot, pe2


def _overlap_matrix(seq):
    n_rows = seq // NSA_CMP_STRIDE
    cs = np.arange(n_rows) * NSA_CMP_STRIDE
    sb = np.arange(LANES) * NSA_SEL_LEN
    ov = np.maximum(np.minimum(cs[:, None] + NSA_CMP_LEN, sb[None, :] + NSA_SEL_LEN)
                    - np.maximum(cs[:, None], sb[None, :]), 0).astype(np.float32) / NSA_CMP_LEN
    ov[:, seq // NSA_SEL_LEN:] = 0.0
    ov[(seq - NSA_CMP_LEN) // NSA_CMP_STRIDE + 1:, :] = 0.0
    return jnp.asarray(ov)


def kernel(x, positions, norm1_g, w_in, mla_q_norm_g, mla_w_uq, mla_kv_norm_g, mla_w_ukv, nsa_cmp_pe, nsa_cmp_w,
           fox_f_bias, w_branch, w_out, norm2_g, w_up, w_down, final_g):
    batch, seq, _ = x.shape
    depth = w_in.shape[0]
    assert seq % TQ == 0 and seq % TM == 0 and seq // NSA_SEL_LEN <= LANES and TQ == TK
    assert seq // NSA_CMP_STRIDE <= LANES and (seq // NSA_CMP_STRIDE) % 8 == 0

    src, scale = _inproj_layout()
    q_src, k_src, v_src = _mla_up_layouts()
    tab = _rope_tables(positions)
    head_mask = _head_masks(N_HEADS, 256)
    consts = _placement_constants(seq)
    eye = jnp.eye(128, dtype=F32)
    overlap = _overlap_matrix(seq)

    x2 = x.reshape(batch * seq, D_MODEL)
    for l in range(depth):
        w_packed = _pack_cols(w_in[l, :, :GATE_BASE], src, scale).astype(BF16)
        wg = w_in[l, :, GATE_BASE:].astype(BF16)
        wuq = _pack_cols(mla_w_uq[l], q_src, np.full(512, (MLA_NOPE + MLA_ROPE) ** -0.5, np.float32)).astype(BF16)
        wuk = _pack_cols(mla_w_ukv[l], k_src, np.ones(512, np.float32)).astype(BF16)
        wuv = jnp.take(mla_w_ukv[l], jnp.asarray(v_src), axis=1).astype(BF16)
        fb = jnp.zeros((1, 128), F32).at[0, :N_HEADS].set(fox_f_bias[l])
        g1 = norm1_g[l][None, :]

        a, bm, kvc, cidx, d, e = _inproj(x2, g1, w_packed, tab, mla_q_norm_g[l][None, :], wuq,
                                          mla_kv_norm_g[l][None, :], wuk, wuv, fb, seq)
        w_top, w_bot, pe2 = _compress_weights(nsa_cmp_w[l], nsa_cmp_pe[l])
        kvcmp = _compress(kvc.reshape(batch, seq // NSA_CMP_STRIDE, NSA_CMP_STRIDE * 128), pe2, w_top, w_bot)

        o_mla = _mla(a, consts, batch, seq)
        o_nsa = _nsa(bm, kvcmp, d, head_mask, overlap, consts, batch, seq)
        o_fox = _fox(e, d, head_mask, eye, consts, batch, seq)
        o_dsa = _dsa(bm, cidx, d, head_mask, consts, batch, seq)

        x2 = _mix(x2, g1, wg, (o_mla, o_nsa, o_fox, o_dsa), w_branch[l].astype(BF16), w_out[l].astype(BF16))
        x2 = _ffn(x2, norm2_g[l][None, :], w_up[l].astype(BF16), w_down[l].astype(BF16), final_g[None, :],
                  l == depth - 1)
    return x2.reshape(batch, seq, D_MODEL)
```

```python
import functools

import numpy as np
import jax
import jax.numpy as jnp
from jax import lax
from jax.experimental import pallas as pl
from jax.experimental.pallas import tpu as pltpu

F32 = jnp.float32
BF16 = jnp.bfloat16

D_MODEL = 1024
ROPE_THETA = 500000.0
NEG_INF = -1e30
NORM_EPS = 1e-6
HEAD_DIM = 64
N_HEADS = 4
D_FF = 4 * D_MODEL
MLA_NOPE, MLA_ROPE, MLA_V = 64, 32, 64
MLA_Q_LORA, MLA_KV_LORA = 256, 128
NSA_CMP_LEN, NSA_CMP_STRIDE, NSA_SEL_LEN = 32, 16, 64
NSA_TOP_N, NSA_WINDOW, NSA_FORCED_SCORE = 8, 512, 1e4
IDX_HEADS, IDX_DIM = 8, 32
DSA_TOP_K, DSA_KEEP_DIV = 256, 4
MLA_COLS, NSA_COLS, FOX_COLS, DSA_COLS = 416, 652, 772, 680
GATE_BASE = MLA_COLS + NSA_COLS + FOX_COLS + DSA_COLS

LANES = 128
TQ = 256
TK = 256
TM = 256
VMEM_LIMIT = 56 * 1024 * 1024
INT_MIN = -2 ** 31

P_CQ, P_CKV, P_KR = 0, 256, 384
P_NQ, P_DQ = 512, 768
P_KVC, P_KVS, P_KVW, P_KVD = 1024, 1152, 1280, 1408
P_QI, P_KI = 1536, 1792
P_G, P_W = 2048, 2176
P_FQ, P_FK, P_FV, P_FF = 2304, 2560, 2816, 3072
P_TOTAL = 3200


def _inproj_layout():
    src = -np.ones(P_TOTAL, np.int64)
    scale = np.ones(P_TOTAL, np.float32)
    mla, nsa, fox, dsa = 0, MLA_COLS, MLA_COLS + NSA_COLS, MLA_COLS + NSA_COLS + FOX_COLS
    ar = np.arange
    src[P_CQ:P_CQ + 256] = mla + ar(256)
    src[P_CKV:P_CKV + 128] = mla + 256 + ar(128)
    src[P_KR + 64:P_KR + 96] = mla + 384 + ar(32)
    src[P_NQ:P_NQ + 256] = nsa + ar(256)
    scale[P_NQ:P_NQ + 256] = HEAD_DIM ** -0.5
    src[P_DQ:P_DQ + 256] = dsa + ar(256)
    scale[P_DQ:P_DQ + 256] = HEAD_DIM ** -0.5
    for base, k_off, v_off, origin in ((P_KVC, 256, 320, nsa), (P_KVS, 384, 448, nsa),
                                       (P_KVW, 512, 576, nsa), (P_KVD, 256, 320, dsa)):
        src[base:base + 64] = origin + k_off + ar(64)
        src[base + 64:base + 128] = origin + v_off + ar(64)
    src[P_QI:P_QI + 256] = dsa + 384 + ar(256)
    scale[P_QI:P_QI + 256] = (IDX_DIM * IDX_HEADS) ** -0.5
    for r in range(IDX_HEADS):
        src[P_KI + 32 * r:P_KI + 32 * r + 32] = dsa + 640 + ar(32)
    src[P_G:P_G + 12] = nsa + 640 + ar(12)
    src[P_W:P_W + 8] = dsa + 672 + ar(8)
    src[P_FQ:P_FQ + 256] = fox + ar(256)
    scale[P_FQ:P_FQ + 256] = HEAD_DIM ** -0.5
    src[P_FK:P_FK + 256] = fox + 256 + ar(256)
    src[P_FV:P_FV + 256] = fox + 512 + ar(256)
    src[P_FF:P_FF + 4] = fox + 768 + ar(4)
    return src, scale


def _pack_cols(w, src, scale):
    pieces, start = [], 0
    for end in range(1, len(src) + 1):
        run_ends = (end == len(src) or scale[end] != scale[start]
                    or (src[end] != src[end - 1] + 1 if src[start] >= 0 else src[end] >= 0))
        if run_ends:
            if src[start] >= 0:
                pieces.append(w[:, int(src[start]):int(src[start]) + end - start] * float(scale[start]))
            else:
                pieces.append(jnp.zeros((w.shape[0], end - start), w.dtype))
            start = end
    return jnp.concatenate(pieces, axis=1)


def _mla_up_layouts():
    q_src = -np.ones(512, np.int64)
    k_src = -np.ones(512, np.int64)
    v_src = np.zeros(256, np.int64)
    for h in range(N_HEADS):
        q_src[h * 128:h * 128 + 96] = h * 96 + np.arange(96)
        k_src[h * 128:h * 128 + 64] = h * 128 + np.arange(64)
        v_src[h * 64:h * 64 + 64] = h * 128 + 64 + np.arange(64)
    return q_src, k_src, v_src


def _rope_tables(positions):
    pos = positions.astype(F32).reshape(-1)[:, None]
    n = pos.shape[0]
    parts, offs = [], {}
    col = 0
    for name, rot in (("mla", MLA_ROPE), ("head", HEAD_DIM // 4), ("idx", IDX_DIM // 4)):
        inv = ROPE_THETA ** (-jnp.arange(0, rot, 2, dtype=F32) / rot)
        ang = pos * inv[None, :]
        parts += [jnp.cos(ang), jnp.sin(ang)]
        offs[name] = (col, col + rot // 2, rot // 2)
        col += rot
    one_col = col
    parts.append(jnp.ones((n, 64 - col), F32))
    cs = jnp.concatenate(parts, axis=1)
    hi = cs.astype(BF16)
    rest = cs - hi.astype(F32)
    mid = rest.astype(BF16)
    low = (rest - mid.astype(F32)).astype(BF16)
    cs3 = jnp.concatenate([hi, mid, low, jnp.zeros((n, 64), BF16)], axis=1)

    place = np.zeros((64, 768), np.float32)

    def fill(cos_base, sin_base, name, group, first, n_groups):
        c0, s0, half = offs[name]
        place[one_col, cos_base:cos_base + 128] = 1.0
        for g in range(n_groups):
            lo = g * group + first
            for j in range(half):
                for lane, sign in ((lo + j, -1.0), (lo + half + j, 1.0)):
                    place[one_col, cos_base + lane] = 0.0
                    place[c0 + j, cos_base + lane] = 1.0
                    place[s0 + j, sin_base + lane] = sign

    fill(0, 128, "mla", 128, 64, 1)
    fill(256, 384, "head", 64, 0, 2)
    fill(512, 640, "idx", 32, 0, 4)
    place3 = np.concatenate([place, place, place, np.zeros((64, 768), np.float32)], axis=0)
    return cs3, jnp.asarray(place3, BF16)


def _head_masks(n_heads, width):
    lane = np.arange(width)
    return jnp.asarray((lane[None, :] // (width // n_heads) == np.arange(n_heads)[:, None]), BF16)


def _placement_constants(seq):
    d = np.arange(64)
    rk = np.zeros((128, 256), np.float32)
    pvs = np.zeros((128, 128), np.float32)
    pv4 = np.zeros((4, 256, 128), np.float32)
    pt4 = np.zeros((4, 128, 256), np.float32)
    pvs[64 + d, d] = 1.0
    for h in range(N_HEADS):
        rk[d, h * 64 + d] = 1.0
        pv4[h, h * 64 + d, d] = 1.0
        pt4[h, d, h * 64 + d] = 1.0
    one_row = np.zeros((1, 128), np.float32)
    one_row[0, 64] = 1.0
    r, c = np.arange(TQ)[:, None], np.arange(TK)[None, :]
    tril = np.where(c <= r, 0.0, NEG_INF).astype(np.float32)
    triu = np.where(c > r, 0.0, NEG_INF).astype(np.float32)
    nc = seq // TK
    key_blk = (np.arange(seq) // NSA_SEL_LEN).reshape(nc, 1, TK)
    expand = (key_blk == np.arange(LANES).reshape(1, LANES, 1)).astype(np.float32)
    bf = lambda a: jnp.asarray(a, BF16)
    return dict(rk=bf(rk), pvs=bf(pvs), pv4=bf(pv4), pt4=bf(pt4), one_row=jnp.asarray(one_row),
                tril=jnp.asarray(tril), triu=jnp.asarray(triu), expand=bf(expand * -NEG_INF),
                eye=bf(np.eye(256, dtype=np.float32)), big_eye=bf(np.eye(256, dtype=np.float32) * -NEG_INF), eyef=jnp.asarray(np.eye(128, dtype=np.float32)),
                trilt=jnp.asarray(np.ascontiguousarray(tril.T)))


def _rms(x, g):
    return x * lax.rsqrt(jnp.mean(x * x, axis=-1, keepdims=True) + NORM_EPS) * g


def _rope_tile(x, cos, sin, is_x1, half):
    fwd = pltpu.roll(x, LANES - half, axis=1)
    bwd = pltpu.roll(x, half, axis=1)
    return x * cos + jnp.where(is_x1, fwd, bwd) * sin


def _nt_dot(a, b):
    return lax.dot_general(a, b, (((1,), (1,)), ((), ())), preferred_element_type=F32)


def _attn_params():
    return pltpu.CompilerParams(dimension_semantics=("arbitrary", "arbitrary"), vmem_limit_bytes=VMEM_LIMIT)


def _row_params():
    return pltpu.CompilerParams(dimension_semantics=("arbitrary",), vmem_limit_bytes=VMEM_LIMIT)


def _inproj_kernel(tiles_per_seq, x_ref, g1_ref, w_ref, cs_ref, place_ref, gq_ref, wuq_ref, gkv_ref, wuk_ref,
                   wuv_ref, fb_ref, a_ref, b_ref, kvc_ref, c_ref, d_ref, e_ref, carry_ref):
    i = pl.program_id(0)
    x = x_ref[...]
    hb = _rms(x, g1_ref[...]).astype(BF16)
    z = jnp.dot(hb, w_ref[...], preferred_element_type=F32)

    lane = lax.broadcasted_iota(jnp.int32, (TM, LANES), 1)
    tab = jnp.dot(cs_ref[...], place_ref[...], preferred_element_type=F32)
    cos1, sin1 = tab[:, 0:128], tab[:, 128:256]
    cos3, sin3 = tab[:, 256:384], tab[:, 384:512]
    cos5, sin5 = tab[:, 512:640], tab[:, 640:768]
    x1_mla = (lane >= 64) & (lane < 80)
    x1_head = (lane & 63) < 8
    x1_idx = (lane & 31) < 4
    k_half = lane < 64
    cos4 = jnp.where(k_half, cos3, 1.0)
    sin4 = jnp.where(k_half, sin3, 0.0)

    cqn = _rms(z[:, P_CQ:P_CQ + 256], gq_ref[...]).astype(BF16)
    q_up = jnp.dot(cqn, wuq_ref[...], preferred_element_type=F32)
    ckvn = _rms(z[:, P_CKV:P_CKV + 128], gkv_ref[...]).astype(BF16)
    k_up = jnp.dot(ckvn, wuk_ref[...], preferred_element_type=F32)
    v_up = jnp.dot(ckvn, wuv_ref[...], preferred_element_type=F32)
    k_rot = _rope_tile(z[:, P_KR:P_KR + 128], cos1, sin1, x1_mla, 16)
    for h in range(N_HEADS):
        sl = slice(h * 128, (h + 1) * 128)
        a_ref[:, sl] = _rope_tile(q_up[:, sl], cos1, sin1, x1_mla, 16).astype(BF16)
        a_ref[:, 512 + h * 128:512 + (h + 1) * 128] = (k_up[:, sl] + k_rot).astype(BF16)
    a_ref[:, 1024:1280] = v_up.astype(BF16)

    for t in range(4):
        b_ref[:, t * 128:(t + 1) * 128] = _rope_tile(
            z[:, P_NQ + t * 128:P_NQ + (t + 1) * 128], cos3, sin3, x1_head, 8).astype(BF16)
    kvc_ref[...] = _rope_tile(z[:, P_KVC:P_KVC + 128], cos4, sin4, x1_head & k_half, 8).astype(BF16)
    for t, off in enumerate((P_KVS, P_KVW, P_KVD)):
        b_ref[:, 512 + t * 128:512 + (t + 1) * 128] = _rope_tile(
            z[:, off:off + 128], cos4, sin4, x1_head & k_half, 8).astype(BF16)

    for t in range(4):
        c_ref[:, t * 128:(t + 1) * 128] = _rope_tile(
            z[:, P_QI + t * 128:P_QI + (t + 1) * 128], cos5, sin5, x1_idx, 4).astype(BF16)

    d_ref[:, 0:128] = 1.0 / (1.0 + jnp.exp(-z[:, P_G:P_G + 128]))
    d_ref[:, 128:256] = z[:, P_W:P_W + 128]
    f = z[:, P_FF:P_FF + 128] + fb_ref[...]
    log_f = jnp.minimum(f, 0.0) - jnp.log(1.0 + jnp.exp(-jnp.abs(f)))

    @pl.when(i % tiles_per_seq == 0)
    def _():
        carry_ref[...] = jnp.zeros_like(carry_ref)

    row = lax.broadcasted_iota(jnp.int32, (TM, TM), 0)
    col = lax.broadcasted_iota(jnp.int32, (TM, TM), 1)
    tri = jnp.where(col <= row, 1.0, 0.0).astype(F32)
    c = jnp.dot(tri, log_f, preferred_element_type=F32, precision=lax.Precision.HIGHEST)
    c = c + carry_ref[0:1, :]
    d_ref[:, 256:384] = c
    carry_ref[...] = jnp.broadcast_to(c[TM - 1:TM, :], carry_ref.shape)

    e_ref[...] = z[:, P_FQ:P_FQ + 768].astype(BF16)


def _inproj(x2, g1, w_packed, cs3, place3, gq, wuq, gkv, wuk, wuv, fb, seq):
    t = x2.shape[0]
    full = lambda shape: pl.BlockSpec(shape, lambda i: (0, 0))
    rows = lambda w: pl.BlockSpec((TM, w), lambda i: (i, 0))
    return pl.pallas_call(
        functools.partial(_inproj_kernel, seq // TM),
        grid=(t // TM,),
        in_specs=[rows(D_MODEL), full((1, D_MODEL)), full((D_MODEL, P_TOTAL)), rows(256), full((256, 768)),
                  full((1, 256)), full((256, 512)), full((1, 128)), full((128, 512)), full((128, 256)),
                  full((1, 128))],
        out_specs=[rows(1280), rows(896), rows(128), rows(512), rows(384), rows(768)],
        out_shape=[jax.ShapeDtypeStruct((t, 1280), BF16), jax.ShapeDtypeStruct((t, 896), BF16),
                   jax.ShapeDtypeStruct((t, 128), BF16), jax.ShapeDtypeStruct((t, 512), BF16),
                   jax.ShapeDtypeStruct((t, 384), F32), jax.ShapeDtypeStruct((t, 768), BF16)],
        scratch_shapes=[pltpu.VMEM((8, 128), F32)],
        compiler_params=_row_params(),
        name="inproj",
    )(x2, g1, w_packed, cs3, place3, gq, wuq, gkv, wuk, wuv, fb)


def _compress_kernel(r_ref, pe_ref, wt_ref, wb_ref, o_ref):
    r = r_ref[0].astype(F32)
    top = jnp.dot((r + pe_ref[0:1, :]).astype(BF16), wt_ref[...], preferred_element_type=F32)
    bot = jnp.dot((r + pe_ref[1:2, :]).astype(BF16), wb_ref[...], preferred_element_type=F32)
    n = bot.shape[0]
    o_ref[0] = (top + pltpu.roll(bot, n - 1, axis=0)).astype(BF16)


def _compress(kvc3, pe2, w_top, w_bot):
    b, n, _ = kvc3.shape
    return pl.pallas_call(
        _compress_kernel,
        grid=(b,),
        in_specs=[pl.BlockSpec((1, n, 2048), lambda i: (i, 0, 0)), pl.BlockSpec((2, 2048), lambda i: (0, 0)),
                  pl.BlockSpec((2048, 128), lambda i: (0, 0)), pl.BlockSpec((2048, 128), lambda i: (0, 0))],
        out_specs=pl.BlockSpec((1, n, 128), lambda i: (i, 0, 0)),
        out_shape=jax.ShapeDtypeStruct((b, n, 128), BF16),
        compiler_params=_row_params(),
        name="nsa_compress",
    )(kvc3, pe2, w_top, w_bot)


def _flash_init(rows):
    return jnp.full((rows, 1), NEG_INF, F32), jnp.zeros((rows, LANES), F32)


def _flash_update(carry, s, v):
    m, acc = carry
    m_new = jnp.maximum(m, jnp.max(s, axis=-1, keepdims=True))
    p = jnp.exp(s - m_new).astype(BF16)
    alpha = jnp.exp(m - m_new)
    return m_new, alpha * acc + jnp.dot(p, v, preferred_element_type=F32)


def _flash_update_many(carries, scores, values):
    ms = [jnp.maximum(m, jnp.max(s, axis=-1, keepdims=True)) for (m, _), s in zip(carries, scores)]
    ps = [jnp.exp(s - m).astype(BF16) for s, m in zip(scores, ms)]
    alphas = [jnp.exp(m_old - m) for (m_old, _), m in zip(carries, ms)]
    pvs = [jnp.dot(p, v, preferred_element_type=F32) for p, v in zip(ps, values)]
    return tuple((m, a * acc + pv) for m, a, (_, acc), pv in zip(ms, alphas, carries, pvs))


def _chunk_loop(n, step, init):
    carry = lax.fori_loop(0, n // 2, lambda c2, cr: step(2 * c2 + 1, step(2 * c2, cr)), init)
    return lax.fori_loop(2 * (n // 2), n, step, carry)


def _flash_out(acc):
    return acc / jnp.maximum(acc[:, 64:65], 1e-30)


def _place_heads(outs, pt_ref):
    res = None
    for h, o in enumerate(outs):
        t = jnp.dot(o.astype(BF16), pt_ref[h], preferred_element_type=F32)
        res = t if res is None else res + t
    return res.astype(BF16)


def _chunk_rows(c):
    return pl.ds(pl.multiple_of(c * TK, TK), TK)


def _stack_heads(x, mask_ref, n):
    return jnp.concatenate([x * mask_ref[h:h + 1, :] for h in range(n)], axis=0)


def _mla_kernel(q_ref, k_ref, v_ref, pv_ref, one_ref, pt_ref, tril_ref, o_ref, vaug_ref):
    i = pl.program_id(1)

    @pl.when(i == 0)
    def _():
        v = v_ref[...]
        for h in range(N_HEADS):
            vaug_ref[h] = (jnp.dot(v, pv_ref[h], preferred_element_type=F32) + one_ref[...]).astype(BF16)

    qs = [q_ref[:, h * 128:(h + 1) * 128] for h in range(N_HEADS)]

    def step(c, carries, bias=None):
        rows = _chunk_rows(c)
        scores = [_nt_dot(qs[h], k_ref[rows, h * 128:(h + 1) * 128]) for h in range(N_HEADS)]
        if bias is not None:
            scores = [s + bias for s in scores]
        return _flash_update_many(carries, scores, [vaug_ref[h, rows, :] for h in range(N_HEADS)])

    carries = _chunk_loop(i, step, tuple(_flash_init(TQ) for _ in range(N_HEADS)))
    carries = step(i, carries, tril_ref[...])
    o_ref[...] = _place_heads([_flash_out(acc) for _, acc in carries], pt_ref)


def _mla(a, k, batch, seq):
    nq = seq // TQ
    const2 = lambda shape: pl.BlockSpec(shape, lambda b, i: (0, 0))
    const3 = lambda shape: pl.BlockSpec(shape, lambda b, i: (0, 0, 0))
    return pl.pallas_call(
        _mla_kernel,
        grid=(batch, nq),
        in_specs=[pl.BlockSpec((TQ, 512), lambda b, i: (b * nq + i, 0)),
                  pl.BlockSpec((seq, 512), lambda b, i: (b, 1)),
                  pl.BlockSpec((seq, 256), lambda b, i: (b, 4)),
                  const3((4, 256, 128)), const2((1, 128)), const3((4, 128, 256)), const2((TQ, TK))],
        out_specs=pl.BlockSpec((TQ, 256), lambda b, i: (b * nq + i, 0)),
        out_shape=jax.ShapeDtypeStruct((batch * seq, 256), BF16),
        scratch_shapes=[pltpu.VMEM((N_HEADS, seq, 128), BF16)],
        compiler_params=_attn_params(),
        name="mla_attn",
    )(a, a, a, k["pv4"], k["one_row"], k["pt4"], k["tril"])


def _fox_kernel(n_chunks, q_ref, k_ref, v_ref, call_ref, hm_ref, eye_ref, pv_ref, one_ref, pt_ref,
                tril_ref, o_ref, vaug_ref, ct_ref):
    i = pl.program_id(1)

    @pl.when(i == 0)
    def _():
        v = v_ref[...]
        for h in range(N_HEADS):
            vaug_ref[h] = (jnp.dot(v, pv_ref[h], preferred_element_type=F32) + one_ref[...]).astype(BF16)
        ct = lax.dot_general(eye_ref[...], call_ref[...], (((1,), (1,)), ((), ())),
                             preferred_element_type=F32, precision=lax.Precision.HIGHEST)
        for c in range(n_chunks):
            ct_ref[c] = ct[0:8, c * TK:(c + 1) * TK]

    q_all = q_ref[...]
    qs = [q_all * hm_ref[h:h + 1, :] for h in range(N_HEADS)]

    def step(c, carries, bias=None):
        rows = _chunk_rows(c)
        k = k_ref[rows, :]
        cs = ct_ref[c]
        scores = [_nt_dot(qs[h], k) - cs[h:h + 1, :] for h in range(N_HEADS)]
        if bias is not None:
            scores = [s + bias for s in scores]
        return _flash_update_many(carries, scores, [vaug_ref[h, rows, :] for h in range(N_HEADS)])

    carries = _chunk_loop(i, step, tuple(_flash_init(TQ) for _ in range(N_HEADS)))
    carries = step(i, carries, tril_ref[...])
    o_ref[...] = _place_heads([_flash_out(acc) for _, acc in carries], pt_ref)


def _fox(e, d, head_mask, eye, k, batch, seq):
    nq = seq // TQ
    nc = seq // TK
    const2 = lambda shape: pl.BlockSpec(shape, lambda b, i: (0, 0))
    const3 = lambda shape: pl.BlockSpec(shape, lambda b, i: (0, 0, 0))
    return pl.pallas_call(
        functools.partial(_fox_kernel, nc),
        grid=(batch, nq),
        in_specs=[pl.BlockSpec((TQ, 256), lambda b, i: (b * nq + i, 0)),
                  pl.BlockSpec((seq, 256), lambda b, i: (b, 1)),
                  pl.BlockSpec((seq, 256), lambda b, i: (b, 2)),
                  pl.BlockSpec((seq, 128), lambda b, i: (b, 2)),
                  const2((N_HEADS, 256)), const2((128, 128)),
                  const3((4, 256, 128)), const2((1, 128)), const3((4, 128, 256)), const2((TQ, TK))],
        out_specs=pl.BlockSpec((TQ, 256), lambda b, i: (b * nq + i, 0)),
        out_shape=jax.ShapeDtypeStruct((batch * seq, 256), BF16),
        scratch_shapes=[pltpu.VMEM((N_HEADS, seq, 128), BF16), pltpu.VMEM((nc, 8, TK), F32)],
        compiler_params=_attn_params(),
        name="fox_attn",
    )(e, e, e, d, head_mask, eye, k["pv4"], k["one_row"], k["pt4"], k["tril"])


def _nsa_kernel(n_sel_blocks, q_ref, kvs_ref, kvw_ref, kvc_ref, g_ref, hm_ref, rk_ref, pvs_ref, one_ref,
                pt_ref, tril_ref, triu_ref, ex_ref, ov_ref, o_ref, ks_ref, vs_ref, kw_ref, vw_ref, kc_ref,
                vc_ref):
    i = pl.program_id(1)

    @pl.when(i == 0)
    def _():
        rk, pvs, one = rk_ref[...], pvs_ref[...], one_ref[...]
        for src_ref, k_dst, v_dst in ((kvs_ref, ks_ref, vs_ref), (kvw_ref, kw_ref, vw_ref)):
            kv = src_ref[...]
            k_dst[...] = jnp.dot(kv, rk, preferred_element_type=F32).astype(BF16)
            v_dst[...] = (jnp.dot(kv, pvs, preferred_element_type=F32) + one).astype(BF16)
        kv = kvc_ref[0]
        kc_ref[...] = jnp.dot(kv, rk, preferred_element_type=F32).astype(BF16)
        vc_ref[...] = jnp.dot(kv, pvs, preferred_element_type=F32).astype(BF16)

    g = g_ref[...]
    q_stack = _stack_heads(q_ref[...], hm_ref, N_HEADS)
    n_cmp_rows = kc_ref.shape[0]
    head = lambda x, h: x[h * TQ:(h + 1) * TQ]

    t_c = i * TQ + lax.broadcasted_iota(jnp.int32, (TQ, n_cmp_rows), 0)
    c_idx = lax.broadcasted_iota(jnp.int32, (TQ, n_cmp_rows), 1)
    cmp_mask = c_idx * NSA_CMP_STRIDE + (NSA_CMP_LEN - 1) <= t_c
    s_cmp = _nt_dot(q_stack, kc_ref[...])
    p_sum = jnp.zeros((TQ, n_cmp_rows), F32)
    o_cmp = []
    for h in range(N_HEADS):
        s = jnp.where(cmp_mask, head(s_cmp, h), NEG_INF)
        m = jnp.max(s, axis=-1, keepdims=True)
        p = jnp.where(cmp_mask, jnp.exp(s - m), 0.0)
        p = p / jnp.maximum(jnp.sum(p, axis=-1, keepdims=True), 1e-30)
        p_sum = p_sum + p
        o_cmp.append(jnp.dot(p.astype(BF16), vc_ref[...], preferred_element_type=F32))

    imp = jnp.dot(p_sum, ov_ref[...], preferred_element_type=F32, precision=lax.Precision.HIGHEST)
    t_r = i * TQ + lax.broadcasted_iota(jnp.int32, (TQ, LANES), 0)
    j = lax.broadcasted_iota(jnp.int32, (TQ, LANES), 1)
    t_blk = t_r >> 6
    forced = (j == 0) | (j == t_blk) | (j == t_blk - 1)
    imp = jnp.where(forced, NSA_FORCED_SCORE, imp)
    imp = jnp.where(j * NSA_SEL_LEN <= t_r, imp, NEG_INF)
    imp = jnp.where(j < n_sel_blocks, imp, -3e38)
    rank = jnp.zeros((TQ, LANES), F32)
    for i2 in range(n_sel_blocks):
        col = imp[:, i2:i2 + 1]
        beats = jnp.where(col > imp, 1.0, jnp.where(col == imp, jnp.where(j > i2, 1.0, 0.0), 0.0))
        rank = rank + beats
    dropped = jnp.where(rank < float(NSA_TOP_N), jnp.where(j < n_sel_blocks, 0.0, -1.0), -1.0).astype(BF16)

    def sel_step(c, carry, bias=None):
        rows = _chunk_rows(c)
        pen = jnp.dot(dropped, ex_ref[c], preferred_element_type=F32)
        if bias is not None:
            pen = pen + bias
        s_all = _nt_dot(q_stack, ks_ref[rows, :])
        s = jnp.concatenate([head(s_all, h) + pen for h in range(N_HEADS)], axis=0)
        return _flash_update(carry, s, vs_ref[rows, :])

    sel = _chunk_loop(i, sel_step, _flash_init(N_HEADS * TQ))
    _, sel_acc = sel_step(i, sel, tril_ref[...])

    win = _flash_init(N_HEADS * TQ)
    for back, bias_ref in ((2, triu_ref), (1, None), (0, tril_ref)):
        c = jnp.maximum(i - back, 0)
        rows = _chunk_rows(c)
        void = jnp.where(i >= back, 0.0, NEG_INF)
        pen = void if bias_ref is None else bias_ref[...] + void
        s_all = _nt_dot(q_stack, kw_ref[rows, :])
        s = jnp.concatenate([head(s_all, h) + pen for h in range(N_HEADS)], axis=0)
        win = _flash_update(win, s, vw_ref[rows, :])
    _, win_acc = win

    o_sel = _flash_out(sel_acc)
    o_win = _flash_out(win_acc)
    outs = []
    for h in range(N_HEADS):
        outs.append(g[:, h:h + 1] * o_cmp[h] + g[:, 4 + h:5 + h] * head(o_sel, h)
                    + g[:, 8 + h:9 + h] * head(o_win, h))
    o_ref[...] = _place_heads(outs, pt_ref)


def _nsa(bm, kvcmp, d, head_mask, overlap, k, batch, seq):
    nq = seq // TQ
    nc = seq // TK
    n_cmp_rows = kvcmp.shape[1]
    const2 = lambda shape: pl.BlockSpec(shape, lambda b, i: (0, 0))
    const3 = lambda shape: pl.BlockSpec(shape, lambda b, i: (0, 0, 0))
    return pl.pallas_call(
        functools.partial(_nsa_kernel, seq // NSA_SEL_LEN),
        grid=(batch, nq),
        in_specs=[pl.BlockSpec((TQ, 256), lambda b, i: (b * nq + i, 0)),
                  pl.BlockSpec((seq, 128), lambda b, i: (b, 4)),
                  pl.BlockSpec((seq, 128), lambda b, i: (b, 5)),
                  pl.BlockSpec((1, n_cmp_rows, 128), lambda b, i: (b, 0, 0)),
                  pl.BlockSpec((TQ, 128), lambda b, i: (b * nq + i, 0)),
                  const2((N_HEADS, 256)), const2((128, 256)), const2((128, 128)), const2((1, 128)),
                  const3((4, 128, 256)), const2((TQ, TK)), const2((TQ, TK)), const3((nc, LANES, TK)),
                  const2((n_cmp_rows, 128))],
        out_specs=pl.BlockSpec((TQ, 256), lambda b, i: (b * nq + i, 0)),
        out_shape=jax.ShapeDtypeStruct((batch * seq, 256), BF16),
        scratch_shapes=[pltpu.VMEM((seq, 256), BF16), pltpu.VMEM((seq, 128), BF16),
                        pltpu.VMEM((seq, 256), BF16), pltpu.VMEM((seq, 128), BF16),
                        pltpu.VMEM((n_cmp_rows, 256), BF16), pltpu.VMEM((n_cmp_rows, 128), BF16)],
        compiler_params=_attn_params(),
        name="nsa_attn",
    )(bm, bm, bm, kvcmp, d, head_mask, k["rk"], k["pvs"], k["one_row"], k["pt4"], k["tril"], k["triu"],
      k["expand"], overlap)


def _dsa_kernel(n_keep, idx_bits, q_ref, kvd_ref, qi_ref, ki_ref, w_ref, hm_ref, rk_ref, pvs_ref,
                one_ref, pt_ref, tril_ref, trilt_ref, eye_ref, eyef_ref, bigeye_ref, o_ref, kd_ref, vd_ref, sc_ref,
                qit_ref):
    i = pl.program_id(1)

    @pl.when(i == 0)
    def _():
        kv = kvd_ref[...]
        kd_ref[...] = jnp.dot(kv, rk_ref[...], preferred_element_type=F32).astype(BF16)
        vd_ref[...] = (jnp.dot(kv, pvs_ref[...], preferred_element_type=F32) + one_ref[...]).astype(BF16)

    head = lambda x, h: x[h * TQ:(h + 1) * TQ]

    qi_t = _nt_dot(eye_ref[...], qi_ref[...])
    feat = lax.broadcasted_iota(jnp.int32, qi_t.shape, 0)
    for h in range(IDX_HEADS):
        qit_ref[:, h * TQ:(h + 1) * TQ] = jnp.where((feat >> 5) == h, qi_t, 0.0).astype(BF16)
    w_t = lax.dot_general(eyef_ref[...], w_ref[...], (((1,), (1,)), ((), ())),
                          preferred_element_type=F32, precision=lax.Precision.HIGHEST)
    ws = [w_t[h:h + 1, :] for h in range(IDX_HEADS)]

    def score_chunk(c, bias):
        logits = jnp.dot(ki_ref[_chunk_rows(c), :], qit_ref[...], preferred_element_type=F32)
        score = jnp.maximum(logits[:, 0:TQ], 0.0) * ws[0]
        for h in range(1, IDX_HEADS):
            score = score + jnp.maximum(logits[:, h * TQ:(h + 1) * TQ], 0.0) * ws[h]
        if bias is not None:
            score = score + bias
        sc_ref[c] = score

    def score_body(c, carry):
        score_chunk(c, None)
        return carry

    lax.fori_loop(0, i, score_body, 0)
    score_chunk(i, trilt_ref[...])

    def count(pred):
        def body(c, part):
            hit = jnp.where(pred(sc_ref[c], c), 1.0, 0.0)
            return part + jnp.sum(hit.reshape(TK // 8, 8, TQ), axis=0)
        part = lax.fori_loop(0, i + 1, body, jnp.zeros((8, TQ), F32))
        return jnp.sum(part, axis=0, keepdims=True)

    def key_to_float(key):
        bits = jnp.where(key < 0, key ^ jnp.int32(0x7FFFFFFF), key)
        return lax.bitcast_convert_type(bits, F32)

    keep = float(n_keep)
    total = ((i + 1) * TK).astype(F32)
    cnt0 = count(lambda s, c: s >= 0.0)
    thr0 = jnp.where(cnt0 >= keep, jnp.int32(0), jnp.int32(INT_MIN))
    cnt0 = jnp.where(cnt0 >= keep, cnt0, total)

    def thr_bit(it, carry):
        thr, cnt_thr = carry
        cand = thr | jnp.left_shift(jnp.int32(1), 30 - it)
        cand_f = key_to_float(cand)
        cnt = count(lambda s, c: s >= cand_f)
        take = cnt >= keep
        return jnp.where(take, cand, thr), jnp.where(take, cnt, cnt_thr)

    thr, cnt_thr = lax.fori_loop(0, 31, thr_bit, (thr0, cnt0))
    thr_f = key_to_float(thr)

    def key_index(c):
        return c * TK + lax.broadcasted_iota(jnp.int32, (TK, TQ), 0)

    def resolve_ties(_):
        need = keep - count(lambda s, c: s > thr_f)

        def cut_bit(it, cut):
            cand = cut | jnp.left_shift(jnp.int32(1), idx_bits - 1 - it)
            cnt = count(lambda s, c: (s == thr_f) & (key_index(c) < cand))
            return jnp.where(cnt < need, cand, cut)

        return lax.fori_loop(0, idx_bits, cut_bit, jnp.zeros((1, TQ), jnp.int32))

    has_excess = jnp.max(jnp.abs(cnt_thr - keep)) > 0.0
    cut = lax.cond(has_excess, resolve_ties, lambda _: jnp.full((1, TQ), 2 ** idx_bits, jnp.int32), 0)

    q_stack = _stack_heads(q_ref[...], hm_ref, N_HEADS)
    big_eye = bigeye_ref[...]

    def attn_step(c, carry, bias=None):
        rows = _chunk_rows(c)
        sc = sc_ref[c]
        tie = jnp.where(key_index(c) <= cut, 0.0, -1.0)
        dropped_t = jnp.where(sc > thr_f, 0.0, jnp.where(sc == thr_f, tie, -1.0)).astype(BF16)
        pen = _nt_dot(big_eye, dropped_t)
        if bias is not None:
            pen = pen + bias
        s_all = _nt_dot(q_stack, kd_ref[rows, :])
        s = jnp.concatenate([head(s_all, h) + pen for h in range(N_HEADS)], axis=0)
        return _flash_update(carry, s, vd_ref[rows, :])

    carry = _chunk_loop(i, attn_step, _flash_init(N_HEADS * TQ))
    _, acc = attn_step(i, carry, tril_ref[...])
    out = _flash_out(acc)
    o_ref[...] = _place_heads([head(out, h) for h in range(N_HEADS)], pt_ref)


def _dsa(bm, cidx, d, head_mask, k, batch, seq):
    nq = seq // TQ
    nc = seq // TK
    n_keep = min(DSA_TOP_K, seq // DSA_KEEP_DIV)
    assert n_keep <= TK
    idx_bits = int(np.log2(seq))
    assert 2 ** idx_bits == seq
    const2 = lambda shape: pl.BlockSpec(shape, lambda b, i: (0, 0))
    const3 = lambda shape: pl.BlockSpec(shape, lambda b, i: (0, 0, 0))
    return pl.pallas_call(
        functools.partial(_dsa_kernel, n_keep, idx_bits),
        grid=(batch, nq),
        in_specs=[pl.BlockSpec((TQ, 256), lambda b, i: (b * nq + i, 1)),
                  pl.BlockSpec((seq, 128), lambda b, i: (b, 6)),
                  pl.BlockSpec((TQ, 256), lambda b, i: (b * nq + i, 0)),
                  pl.BlockSpec((seq, 256), lambda b, i: (b, 1)),
                  pl.BlockSpec((TQ, 128), lambda b, i: (b * nq + i, 1)),
                  const2((N_HEADS, 256)), const2((128, 256)), const2((128, 128)),
                  const2((1, 128)), const3((4, 128, 256)), const2((TQ, TK)), const2((TK, TQ)),
                  const2((256, 256)), const2((128, 128)), const2((256, 256))],
        out_specs=pl.BlockSpec((TQ, 256), lambda b, i: (b * nq + i, 0)),
        out_shape=jax.ShapeDtypeStruct((batch * seq, 256), BF16),
        scratch_shapes=[pltpu.VMEM((seq, 256), BF16), pltpu.VMEM((seq, 128), BF16),
                        pltpu.VMEM((nc, TK, TQ), F32), pltpu.VMEM((256, IDX_HEADS * TQ), BF16)],
        compiler_params=_attn_params(),
        name="dsa_attn",
    )(bm, bm, cidx, cidx, d, head_mask, k["rk"], k["pvs"], k["one_row"], k["pt4"], k["tril"], k["trilt"],
      k["eye"], k["eyef"], k["big_eye"])


def _mix_kernel(x_ref, g1_ref, wg_ref, o0_ref, o1_ref, o2_ref, o3_ref, wb_ref, wo_ref, y_ref):
    x = x_ref[...]
    hb = _rms(x, g1_ref[...]).astype(BF16)
    mixed = jnp.zeros((TM, D_MODEL), F32)
    for n, o_ref in enumerate((o0_ref, o1_ref, o2_ref, o3_ref)):
        zg = jnp.dot(hb, wg_ref[:, n * D_MODEL:(n + 1) * D_MODEL], preferred_element_type=F32)
        gate = 1.0 / (1.0 + jnp.exp(-zg))
        lifted = jnp.dot(o_ref[...], wb_ref[n], preferred_element_type=F32)
        mixed = mixed + gate * lifted
    y_ref[...] = x + jnp.dot(mixed.astype(BF16), wo_ref[...], preferred_element_type=F32)


def _mix(x2, g1, wg, outs, wb, wo):
    t = x2.shape[0]
    rows = lambda w: pl.BlockSpec((TM, w), lambda i: (i, 0))
    return pl.pallas_call(
        _mix_kernel,
        grid=(t // TM,),
        in_specs=[rows(D_MODEL), pl.BlockSpec((1, D_MODEL), lambda i: (0, 0)),
                  pl.BlockSpec((D_MODEL, 4 * D_MODEL), lambda i: (0, 0)),
                  rows(256), rows(256), rows(256), rows(256),
                  pl.BlockSpec((4, 256, D_MODEL), lambda i: (0, 0, 0)),
                  pl.BlockSpec((D_MODEL, D_MODEL), lambda i: (0, 0))],
        out_specs=rows(D_MODEL),
        out_shape=jax.ShapeDtypeStruct((t, D_MODEL), F32),
        compiler_params=_row_params(),
        name="gate_mix",
    )(x2, g1, wg, *outs, wb, wo)


def _ffn_kernel(final, x_ref, g2_ref, wu_ref, wd_ref, gf_ref, y_ref):
    x = x_ref[...]
    hb = _rms(x, g2_ref[...]).astype(BF16)
    acc = jnp.zeros((TM, D_MODEL), F32)
    for jc in range(D_FF // D_MODEL):
        u = jnp.dot(hb, wu_ref[:, jc * D_MODEL:(jc + 1) * D_MODEL], preferred_element_type=F32)
        u = jnp.square(jnp.maximum(u, 0.0)).astype(BF16)
        acc = acc + jnp.dot(u, wd_ref[jc * D_MODEL:(jc + 1) * D_MODEL, :], preferred_element_type=F32)
    y = x + acc
    if final:
        y = _rms(y, gf_ref[...])
    y_ref[...] = y


def _ffn(x2, g2, wu, wd, gf, final):
    t = x2.shape[0]
    rows = pl.BlockSpec((TM, D_MODEL), lambda i: (i, 0))
    vec = pl.BlockSpec((1, D_MODEL), lambda i: (0, 0))
    return pl.pallas_call(
        functools.partial(_ffn_kernel, final),
        grid=(t // TM,),
        in_specs=[rows, vec, pl.BlockSpec((D_MODEL, D_FF), lambda i: (0, 0)),
                  pl.BlockSpec((D_FF, D_MODEL), lambda i: (0, 0)), vec],
        out_specs=rows,
        out_shape=jax.ShapeDtypeStruct((t, D_MODEL), F32),
        compiler_params=_row_params(),
        name="ffn",
    )(x2, g2, wu, wd, gf)


def _compress_weights(cmp_w, cmp_pe):
    wk = cmp_w[0].reshape(NSA_CMP_LEN, HEAD_DIM, HEAD_DIM)
    wv = cmp_w[1].reshape(NSA_CMP_LEN, HEAD_DIM, HEAD_DIM)
    zero = jnp.zeros_like(wk)
    full = jnp.concatenate([jnp.concatenate([wk, zero], axis=2), jnp.concatenate([zero, wv], axis=2)], axis=1)
    half = NSA_CMP_LEN // 2
    w_top = full[:half].reshape(half * 128, 128).astype(BF16)
    w_bot = full[half:].reshape(half * 128, 128).astype(BF16)
    pe = jnp.concatenate([cmp_pe[0], cmp_pe[1]], axis=1)
    pe2 = pe.reshape(2, half * 128)
    return w_top, w_bot, pe2


def _overlap_matrix(seq):
    n_rows = seq // NSA_CMP_STRIDE
    cs = np.arange(n_rows) * NSA_CMP_STRIDE
    sb = np.arange(LANES) * NSA_SEL_LEN
    ov = np.maximum(np.minimum(cs[:, None] + NSA_CMP_LEN, sb[None, :] + NSA_SEL_LEN)
                    - np.maximum(cs[:, None], sb[None, :]), 0).astype(np.float32) / NSA_CMP_LEN
    ov[:, seq // NSA_SEL_LEN:] = 0.0
    ov[(seq - NSA_CMP_LEN) // NSA_CMP_STRIDE + 1:, :] = 0.0
    return jnp.asarray(ov)


def kernel(x, positions, norm1_g, w_in, mla_q_norm_g, mla_w_uq, mla_kv_norm_g, mla_w_ukv, nsa_cmp_pe, nsa_cmp_w,
           fox_f_bias, w_branch, w_out, norm2_g, w_up, w_down, final_g):
    batch, seq, _ = x.shape
    depth = w_in.shape[0]
    assert seq % TQ == 0 and seq % TM == 0 and seq // NSA_SEL_LEN <= LANES and TQ == TK
    assert seq // NSA_CMP_STRIDE <= LANES and (seq // NSA_CMP_STRIDE) % 8 == 0

    src, scale = _inproj_layout()
    q_src, k_src, v_src = _mla_up_layouts()
    cs3, place3 = _rope_tables(positions)
    head_mask = _head_masks(N_HEADS, 256)
    consts = _placement_constants(seq)
    eye = jnp.eye(128, dtype=F32)
    overlap = _overlap_matrix(seq)

    x2 = x.reshape(batch * seq, D_MODEL)
    for l in range(depth):
        w_packed = _pack_cols(w_in[l, :, :GATE_BASE], src, scale).astype(BF16)
        wg = w_in[l, :, GATE_BASE:].astype(BF16)
        wuq = _pack_cols(mla_w_uq[l], q_src, np.full(512, (MLA_NOPE + MLA_ROPE) ** -0.5, np.float32)).astype(BF16)
        wuk = _pack_cols(mla_w_ukv[l], k_src, np.ones(512, np.float32)).astype(BF16)
        wuv = _pack_cols(mla_w_ukv[l], v_src, np.ones(256, np.float32)).astype(BF16)
        fb = jnp.zeros((1, 128), F32).at[0, :N_HEADS].set(fox_f_bias[l])
        g1 = norm1_g[l][None, :]

        a, bm, kvc, cidx, d, e = _inproj(x2, g1, w_packed, cs3, place3, mla_q_norm_g[l][None, :], wuq,
                                          mla_kv_norm_g[l][None, :], wuk, wuv, fb, seq)
        w_top, w_bot, pe2 = _compress_weights(nsa_cmp_w[l], nsa_cmp_pe[l])
        kvcmp = _compress(kvc.reshape(batch, seq // NSA_CMP_STRIDE, NSA_CMP_STRIDE * 128), pe2, w_top, w_bot)

        o_mla = _mla(a, consts, batch, seq)
        o_nsa = _nsa(bm, kvcmp, d, head_mask, overlap, consts, batch, seq)
        o_fox = _fox(e, d, head_mask, eye, consts, batch, seq)
        o_dsa = _dsa(bm, cidx, d, head_mask, consts, batch, seq)

        x2 = _mix(x2, g1, wg, (o_mla, o_nsa, o_fox, o_dsa), w_branch[l].astype(BF16), w_out[l].astype(BF16))
        x2 = _ffn(x2, norm2_g[l][None, :], w_up[l].astype(BF16), w_down[l].astype(BF16), final_g[None, :],
                  l == depth - 1)
    return x2.reshape(batch, seq, D_MODEL)
```

```python
import functools

import numpy as np
import jax
import jax.numpy as jnp
from jax import lax
from jax.experimental import pallas as pl
from jax.experimental.pallas import tpu as pltpu

F32 = jnp.float32
BF16 = jnp.bfloat16

D_MODEL = 1024
ROPE_THETA = 500000.0
NEG_INF = -1e30
NORM_EPS = 1e-6
HEAD_DIM = 64
N_HEADS = 4
D_FF = 4 * D_MODEL
MLA_NOPE, MLA_ROPE, MLA_V = 64, 32, 64
MLA_Q_LORA, MLA_KV_LORA = 256, 128
NSA_CMP_LEN, NSA_CMP_STRIDE, NSA_SEL_LEN = 32, 16, 64
NSA_TOP_N, NSA_WINDOW, NSA_FORCED_SCORE = 8, 512, 1e4
IDX_HEADS, IDX_DIM = 8, 32
DSA_TOP_K, DSA_KEEP_DIV = 256, 4
MLA_COLS, NSA_COLS, FOX_COLS, DSA_COLS = 416, 652, 772, 680
GATE_BASE = MLA_COLS + NSA_COLS + FOX_COLS + DSA_COLS

LANES = 128
TQ = 256
TK = 256
TM = 256
VMEM_LIMIT = 56 * 1024 * 1024
INT_MIN = -2 ** 31

P_CQ, P_CKV, P_KR = 0, 256, 384
P_NQ, P_DQ = 512, 768
P_KVC, P_KVS, P_KVW, P_KVD = 1024, 1152, 1280, 1408
P_QI, P_KI = 1536, 1792
P_G, P_W = 2048, 2176
P_FQ, P_FK, P_FV, P_FF = 2304, 2560, 2816, 3072
P_TOTAL = 3200


def _inproj_layout():
    src = -np.ones(P_TOTAL, np.int64)
    scale = np.ones(P_TOTAL, np.float32)
    mla, nsa, fox, dsa = 0, MLA_COLS, MLA_COLS + NSA_COLS, MLA_COLS + NSA_COLS + FOX_COLS
    ar = np.arange
    src[P_CQ:P_CQ + 256] = mla + ar(256)
    src[P_CKV:P_CKV + 128] = mla + 256 + ar(128)
    src[P_KR + 64:P_KR + 96] = mla + 384 + ar(32)
    src[P_NQ:P_NQ + 256] = nsa + ar(256)
    scale[P_NQ:P_NQ + 256] = HEAD_DIM ** -0.5
    src[P_DQ:P_DQ + 256] = dsa + ar(256)
    scale[P_DQ:P_DQ + 256] = HEAD_DIM ** -0.5
    for base, k_off, v_off, origin in ((P_KVC, 256, 320, nsa), (P_KVS, 384, 448, nsa),
                                       (P_KVW, 512, 576, nsa), (P_KVD, 256, 320, dsa)):
        src[base:base + 64] = origin + k_off + ar(64)
        src[base + 64:base + 128] = origin + v_off + ar(64)
    src[P_QI:P_QI + 256] = dsa + 384 + ar(256)
    scale[P_QI:P_QI + 256] = (IDX_DIM * IDX_HEADS) ** -0.5
    for r in range(IDX_HEADS):
        src[P_KI + 32 * r:P_KI + 32 * r + 32] = dsa + 640 + ar(32)
    src[P_G:P_G + 12] = nsa + 640 + ar(12)
    src[P_W:P_W + 8] = dsa + 672 + ar(8)
    src[P_FQ:P_FQ + 256] = fox + ar(256)
    scale[P_FQ:P_FQ + 256] = HEAD_DIM ** -0.5
    src[P_FK:P_FK + 256] = fox + 256 + ar(256)
    src[P_FV:P_FV + 256] = fox + 512 + ar(256)
    src[P_FF:P_FF + 4] = fox + 768 + ar(4)
    return src, scale


def _pack_cols(w, src, scale):
    pieces, start = [], 0
    for end in range(1, len(src) + 1):
        run_ends = (end == len(src) or scale[end] != scale[start]
                    or (src[end] != src[end - 1] + 1 if src[start] >= 0 else src[end] >= 0))
        if run_ends:
            if src[start] >= 0:
                pieces.append(w[:, int(src[start]):int(src[start]) + end - start] * float(scale[start]))
            else:
                pieces.append(jnp.zeros((w.shape[0], end - start), w.dtype))
            start = end
    return jnp.concatenate(pieces, axis=1)


def _mla_up_layouts():
    q_src = -np.ones(512, np.int64)
    k_src = -np.ones(512, np.int64)
    v_src = np.zeros(256, np.int64)
    for h in range(N_HEADS):
        q_src[h * 128:h * 128 + 96] = h * 96 + np.arange(96)
        k_src[h * 128:h * 128 + 64] = h * 128 + np.arange(64)
        v_src[h * 64:h * 64 + 64] = h * 128 + 64 + np.arange(64)
    return q_src, k_src, v_src


def _rope_tables(positions):
    pos = positions.astype(F32).reshape(-1)[None, :]
    n = pos.shape[1]
    parts, offs = [], {}
    col = 0
    for name, rot in (("mla", MLA_ROPE), ("head", HEAD_DIM // 4), ("idx", IDX_DIM // 4)):
        inv = ROPE_THETA ** (-jnp.arange(0, rot, 2, dtype=F32) / rot)
        ang = inv[:, None] * pos
        parts += [jnp.cos(ang), jnp.sin(ang)]
        offs[name] = (col, col + rot // 2, rot // 2)
        col += rot
    one_col = col
    parts.append(jnp.ones((64 - col, n), F32))
    cs = jnp.concatenate(parts, axis=0)
    hi = cs.astype(BF16)
    rest = cs - hi.astype(F32)
    mid = rest.astype(BF16)
    low = (rest - mid.astype(F32)).astype(BF16)
    cs3 = jnp.concatenate([hi, mid, low, jnp.zeros((64, n), BF16)], axis=0).T

    place = np.zeros((64, 768), np.float32)

    def fill(cos_base, sin_base, name, group, first, n_groups):
        c0, s0, half = offs[name]
        place[one_col, cos_base:cos_base + 128] = 1.0
        for g in range(n_groups):
            lo = g * group + first
            for j in range(half):
                for lane, sign in ((lo + j, -1.0), (lo + half + j, 1.0)):
                    place[one_col, cos_base + lane] = 0.0
                    place[c0 + j, cos_base + lane] = 1.0
                    place[s0 + j, sin_base + lane] = sign

    fill(0, 128, "mla", 128, 64, 1)
    fill(256, 384, "head", 64, 0, 2)
    fill(512, 640, "idx", 32, 0, 4)
    place3 = np.concatenate([place, place, place, np.zeros((64, 768), np.float32)], axis=0)
    return cs3, jnp.asarray(place3, BF16)


def _head_masks(n_heads, width):
    lane = np.arange(width)
    return jnp.asarray((lane[None, :] // (width // n_heads) == np.arange(n_heads)[:, None]), BF16)


def _placement_constants(seq):
    d = np.arange(64)
    rk = np.zeros((128, 256), np.float32)
    pvs = np.zeros((128, 128), np.float32)
    pv4 = np.zeros((4, 256, 128), np.float32)
    pt4 = np.zeros((4, 128, 256), np.float32)
    pvs[64 + d, d] = 1.0
    for h in range(N_HEADS):
        rk[d, h * 64 + d] = 1.0
        pv4[h, h * 64 + d, d] = 1.0
        pt4[h, d, h * 64 + d] = 1.0
    one_row = np.zeros((1, 128), np.float32)
    one_row[0, 64] = 1.0
    r, c = np.arange(TQ)[:, None], np.arange(TK)[None, :]
    tril = np.where(c <= r, 0.0, NEG_INF).astype(np.float32)
    triu = np.where(c > r, 0.0, NEG_INF).astype(np.float32)
    nc = seq // TK
    key_blk = (np.arange(seq) // NSA_SEL_LEN).reshape(nc, 1, TK)
    expand = (key_blk == np.arange(LANES).reshape(1, LANES, 1)).astype(np.float32)
    bf = lambda a: jnp.asarray(a, BF16)
    return dict(rk=bf(rk), pvs=bf(pvs), pv4=bf(pv4), pt4=bf(pt4), one_row=jnp.asarray(one_row),
                tril=jnp.asarray(tril), triu=jnp.asarray(triu), expand=bf(expand * -NEG_INF),
                eye=bf(np.eye(256, dtype=np.float32)), big_eye=bf(np.eye(256, dtype=np.float32) * -NEG_INF), eyef=jnp.asarray(np.eye(128, dtype=np.float32)),
                trilt=jnp.asarray(np.ascontiguousarray(tril.T)))


def _rms(x, g):
    return x * lax.rsqrt(jnp.mean(x * x, axis=-1, keepdims=True) + NORM_EPS) * g


def _rope_tile(x, cos, sin, is_x1, half):
    fwd = pltpu.roll(x, LANES - half, axis=1)
    bwd = pltpu.roll(x, half, axis=1)
    return x * cos + jnp.where(is_x1, fwd, bwd) * sin


def _nt_dot(a, b):
    return lax.dot_general(a, b, (((1,), (1,)), ((), ())), preferred_element_type=F32)


def _attn_params():
    return pltpu.CompilerParams(dimension_semantics=("arbitrary", "arbitrary"), vmem_limit_bytes=VMEM_LIMIT)


def _row_params():
    return pltpu.CompilerParams(dimension_semantics=("arbitrary",), vmem_limit_bytes=VMEM_LIMIT)


def _inproj_kernel(tiles_per_seq, x_ref, g1_ref, w_ref, cs_ref, place_ref, gq_ref, wuq_ref, gkv_ref, wuk_ref,
                   wuv_ref, fb_ref, a_ref, b_ref, kvc_ref, c_ref, d_ref, e_ref, carry_ref):
    i = pl.program_id(0)
    x = x_ref[...]
    hb = _rms(x, g1_ref[...]).astype(BF16)
    z = jnp.dot(hb, w_ref[...], preferred_element_type=F32)

    lane = lax.broadcasted_iota(jnp.int32, (TM, LANES), 1)
    tab = jnp.dot(cs_ref[...], place_ref[...], preferred_element_type=F32)
    cos1, sin1 = tab[:, 0:128], tab[:, 128:256]
    cos3, sin3 = tab[:, 256:384], tab[:, 384:512]
    cos5, sin5 = tab[:, 512:640], tab[:, 640:768]
    x1_mla = (lane >= 64) & (lane < 80)
    x1_head = (lane & 63) < 8
    x1_idx = (lane & 31) < 4
    k_half = lane < 64
    cos4 = jnp.where(k_half, cos3, 1.0)
    sin4 = jnp.where(k_half, sin3, 0.0)

    cqn = _rms(z[:, P_CQ:P_CQ + 256], gq_ref[...]).astype(BF16)
    q_up = jnp.dot(cqn, wuq_ref[...], preferred_element_type=F32)
    ckvn = _rms(z[:, P_CKV:P_CKV + 128], gkv_ref[...]).astype(BF16)
    k_up = jnp.dot(ckvn, wuk_ref[...], preferred_element_type=F32)
    v_up = jnp.dot(ckvn, wuv_ref[...], preferred_element_type=F32)
    k_rot = _rope_tile(z[:, P_KR:P_KR + 128], cos1, sin1, x1_mla, 16)
    for h in range(N_HEADS):
        sl = slice(h * 128, (h + 1) * 128)
        a_ref[:, sl] = _rope_tile(q_up[:, sl], cos1, sin1, x1_mla, 16).astype(BF16)
        a_ref[:, 512 + h * 128:512 + (h + 1) * 128] = (k_up[:, sl] + k_rot).astype(BF16)
    a_ref[:, 1024:1280] = v_up.astype(BF16)

    for t in range(4):
        b_ref[:, t * 128:(t + 1) * 128] = _rope_tile(
            z[:, P_NQ + t * 128:P_NQ + (t + 1) * 128], cos3, sin3, x1_head, 8).astype(BF16)
    kvc_ref[...] = _rope_tile(z[:, P_KVC:P_KVC + 128], cos4, sin4, x1_head & k_half, 8).astype(BF16)
    for t, off in enumerate((P_KVS, P_KVW, P_KVD)):
        b_ref[:, 512 + t * 128:512 + (t + 1) * 128] = _rope_tile(
            z[:, off:off + 128], cos4, sin4, x1_head & k_half, 8).astype(BF16)

    for t in range(4):
        c_ref[:, t * 128:(t + 1) * 128] = _rope_tile(
            z[:, P_QI + t * 128:P_QI + (t + 1) * 128], cos5, sin5, x1_idx, 4).astype(BF16)

    d_ref[:, 0:128] = 1.0 / (1.0 + jnp.exp(-z[:, P_G:P_G + 128]))
    d_ref[:, 128:256] = z[:, P_W:P_W + 128]
    f = z[:, P_FF:P_FF + 128] + fb_ref[...]
    log_f = jnp.minimum(f, 0.0) - jnp.log(1.0 + jnp.exp(-jnp.abs(f)))

    @pl.when(i % tiles_per_seq == 0)
    def _():
        carry_ref[...] = jnp.zeros_like(carry_ref)

    row = lax.broadcasted_iota(jnp.int32, (TM, TM), 0)
    col = lax.broadcasted_iota(jnp.int32, (TM, TM), 1)
    tri = jnp.where(col <= row, 1.0, 0.0).astype(F32)
    c = jnp.dot(tri, log_f, preferred_element_type=F32, precision=lax.Precision.HIGHEST)
    c = c + carry_ref[0:1, :]
    d_ref[:, 256:384] = c
    carry_ref[...] = jnp.broadcast_to(c[TM - 1:TM, :], carry_ref.shape)

    e_ref[...] = z[:, P_FQ:P_FQ + 768].astype(BF16)


def _inproj(x2, g1, w_packed, cs3, place3, gq, wuq, gkv, wuk, wuv, fb, seq):
    t = x2.shape[0]
    full = lambda shape: pl.BlockSpec(shape, lambda i: (0, 0))
    rows = lambda w: pl.BlockSpec((TM, w), lambda i: (i, 0))
    return pl.pallas_call(
        functools.partial(_inproj_kernel, seq // TM),
        grid=(t // TM,),
        in_specs=[rows(D_MODEL), full((1, D_MODEL)), full((D_MODEL, P_TOTAL)), rows(256), full((256, 768)),
                  full((1, 256)), full((256, 512)), full((1, 128)), full((128, 512)), full((128, 256)),
                  full((1, 128))],
        out_specs=[rows(1280), rows(896), rows(128), rows(512), rows(384), rows(768)],
        out_shape=[jax.ShapeDtypeStruct((t, 1280), BF16), jax.ShapeDtypeStruct((t, 896), BF16),
                   jax.ShapeDtypeStruct((t, 128), BF16), jax.ShapeDtypeStruct((t, 512), BF16),
                   jax.ShapeDtypeStruct((t, 384), F32), jax.ShapeDtypeStruct((t, 768), BF16)],
        scratch_shapes=[pltpu.VMEM((8, 128), F32)],
        compiler_params=_row_params(),
        name="inproj",
    )(x2, g1, w_packed, cs3, place3, gq, wuq, gkv, wuk, wuv, fb)


def _compress_kernel(r_ref, pe_ref, wt_ref, wb_ref, o_ref):
    r = r_ref[0].astype(F32)
    top = jnp.dot((r + pe_ref[0:1, :]).astype(BF16), wt_ref[...], preferred_element_type=F32)
    bot = jnp.dot((r + pe_ref[1:2, :]).astype(BF16), wb_ref[...], preferred_element_type=F32)
    n = bot.shape[0]
    o_ref[0] = (top + pltpu.roll(bot, n - 1, axis=0)).astype(BF16)


def _compress(kvc3, pe2, w_top, w_bot):
    b, n, _ = kvc3.shape
    return pl.pallas_call(
        _compress_kernel,
        grid=(b,),
        in_specs=[pl.BlockSpec((1, n, 2048), lambda i: (i, 0, 0)), pl.BlockSpec((2, 2048), lambda i: (0, 0)),
                  pl.BlockSpec((2048, 128), lambda i: (0, 0)), pl.BlockSpec((2048, 128), lambda i: (0, 0))],
        out_specs=pl.BlockSpec((1, n, 128), lambda i: (i, 0, 0)),
        out_shape=jax.ShapeDtypeStruct((b, n, 128), BF16),
        compiler_params=_row_params(),
        name="nsa_compress",
    )(kvc3, pe2, w_top, w_bot)


def _flash_init(rows):
    return jnp.full((rows, 1), NEG_INF, F32), jnp.zeros((rows, LANES), F32)


def _flash_update(carry, s, v):
    m, acc = carry
    m_new = jnp.maximum(m, jnp.max(s, axis=-1, keepdims=True))
    p = jnp.exp(s - m_new).astype(BF16)
    alpha = jnp.exp(m - m_new)
    return m_new, alpha * acc + jnp.dot(p, v, preferred_element_type=F32)


def _flash_update_many(carries, scores, values):
    ms = [jnp.maximum(m, jnp.max(s, axis=-1, keepdims=True)) for (m, _), s in zip(carries, scores)]
    ps = [jnp.exp(s - m).astype(BF16) for s, m in zip(scores, ms)]
    alphas = [jnp.exp(m_old - m) for (m_old, _), m in zip(carries, ms)]
    pvs = [jnp.dot(p, v, preferred_element_type=F32) for p, v in zip(ps, values)]
    return tuple((m, a * acc + pv) for m, a, (_, acc), pv in zip(ms, alphas, carries, pvs))


def _chunk_loop(n, step, init):
    carry = lax.fori_loop(0, n // 2, lambda c2, cr: step(2 * c2 + 1, step(2 * c2, cr)), init)
    return lax.fori_loop(2 * (n // 2), n, step, carry)


def _flash_out(acc):
    return acc / jnp.maximum(acc[:, 64:65], 1e-30)


def _place_heads(outs, pt_ref):
    res = None
    for h, o in enumerate(outs):
        t = jnp.dot(o.astype(BF16), pt_ref[h], preferred_element_type=F32)
        res = t if res is None else res + t
    return res.astype(BF16)


def _chunk_rows(c):
    return pl.ds(pl.multiple_of(c * TK, TK), TK)


def _stack_heads(x, mask_ref, n):
    return jnp.concatenate([x * mask_ref[h:h + 1, :] for h in range(n)], axis=0)


def _mla_kernel(q_ref, k_ref, v_ref, pv_ref, one_ref, pt_ref, tril_ref, o_ref, vaug_ref):
    i = pl.program_id(1)

    @pl.when(i == 0)
    def _():
        v = v_ref[...]
        for h in range(N_HEADS):
            vaug_ref[h] = (jnp.dot(v, pv_ref[h], preferred_element_type=F32) + one_ref[...]).astype(BF16)

    qs = [q_ref[:, h * 128:(h + 1) * 128] for h in range(N_HEADS)]

    def step(c, carries, bias=None):
        rows = _chunk_rows(c)
        scores = [_nt_dot(qs[h], k_ref[rows, h * 128:(h + 1) * 128]) for h in range(N_HEADS)]
        if bias is not None:
            scores = [s + bias for s in scores]
        return _flash_update_many(carries, scores, [vaug_ref[h, rows, :] for h in range(N_HEADS)])

    carries = _chunk_loop(i, step, tuple(_flash_init(TQ) for _ in range(N_HEADS)))
    carries = step(i, carries, tril_ref[...])
    o_ref[...] = _place_heads([_flash_out(acc) for _, acc in carries], pt_ref)


def _mla(a, k, batch, seq):
    nq = seq // TQ
    const2 = lambda shape: pl.BlockSpec(shape, lambda b, i: (0, 0))
    const3 = lambda shape: pl.BlockSpec(shape, lambda b, i: (0, 0, 0))
    return pl.pallas_call(
        _mla_kernel,
        grid=(batch, nq),
        in_specs=[pl.BlockSpec((TQ, 512), lambda b, i: (b * nq + i, 0)),
                  pl.BlockSpec((seq, 512), lambda b, i: (b, 1)),
                  pl.BlockSpec((seq, 256), lambda b, i: (b, 4)),
                  const3((4, 256, 128)), const2((1, 128)), const3((4, 128, 256)), const2((TQ, TK))],
        out_specs=pl.BlockSpec((TQ, 256), lambda b, i: (b * nq + i, 0)),
        out_shape=jax.ShapeDtypeStruct((batch * seq, 256), BF16),
        scratch_shapes=[pltpu.VMEM((N_HEADS, seq, 128), BF16)],
        compiler_params=_attn_params(),
        name="mla_attn",
    )(a, a, a, k["pv4"], k["one_row"], k["pt4"], k["tril"])


def _fox_kernel(n_chunks, q_ref, k_ref, v_ref, call_ref, hm_ref, eye_ref, pv_ref, one_ref, pt_ref,
                tril_ref, o_ref, vaug_ref, ct_ref):
    i = pl.program_id(1)

    @pl.when(i == 0)
    def _():
        v = v_ref[...]
        for h in range(N_HEADS):
            vaug_ref[h] = (jnp.dot(v, pv_ref[h], preferred_element_type=F32) + one_ref[...]).astype(BF16)
        ct = lax.dot_general(eye_ref[...], call_ref[...], (((1,), (1,)), ((), ())),
                             preferred_element_type=F32, precision=lax.Precision.HIGHEST)
        for c in range(n_chunks):
            ct_ref[c] = ct[0:8, c * TK:(c + 1) * TK]

    q_all = q_ref[...]
    qs = [q_all * hm_ref[h:h + 1, :] for h in range(N_HEADS)]

    def step(c, carries, bias=None):
        rows = _chunk_rows(c)
        k = k_ref[rows, :]
        cs = ct_ref[c]
        scores = [_nt_dot(qs[h], k) - cs[h:h + 1, :] for h in range(N_HEADS)]
        if bias is not None:
            scores = [s + bias for s in scores]
        return _flash_update_many(carries, scores, [vaug_ref[h, rows, :] for h in range(N_HEADS)])

    carries = _chunk_loop(i, step, tuple(_flash_init(TQ) for _ in range(N_HEADS)))
    carries = step(i, carries, tril_ref[...])
    o_ref[...] = _place_heads([_flash_out(acc) for _, acc in carries], pt_ref)


def _fox(e, d, head_mask, eye, k, batch, seq):
    nq = seq // TQ
    nc = seq // TK
    const2 = lambda shape: pl.BlockSpec(shape, lambda b, i: (0, 0))
    const3 = lambda shape: pl.BlockSpec(shape, lambda b, i: (0, 0, 0))
    return pl.pallas_call(
        functools.partial(_fox_kernel, nc),
        grid=(batch, nq),
        in_specs=[pl.BlockSpec((TQ, 256), lambda b, i: (b * nq + i, 0)),
                  pl.BlockSpec((seq, 256), lambda b, i: (b, 1)),
                  pl.BlockSpec((seq, 256), lambda b, i: (b, 2)),
                  pl.BlockSpec((seq, 128), lambda b, i: (b, 2)),
                  const2((N_HEADS, 256)), const2((128, 128)),
                  const3((4, 256, 128)), const2((1, 128)), const3((4, 128, 256)), const2((TQ, TK))],
        out_specs=pl.BlockSpec((TQ, 256), lambda b, i: (b * nq + i, 0)),
        out_shape=jax.ShapeDtypeStruct((batch * seq, 256), BF16),
        scratch_shapes=[pltpu.VMEM((N_HEADS, seq, 128), BF16), pltpu.VMEM((nc, 8, TK), F32)],
        compiler_params=_attn_params(),
        name="fox_attn",
    )(e, e, e, d, head_mask, eye, k["pv4"], k["one_row"], k["pt4"], k["tril"])


def _nsa_kernel(n_sel_blocks, q_ref, kvs_ref, kvw_ref, kvc_ref, g_ref, hm_ref, rk_ref, pvs_ref, one_ref,
                pt_ref, tril_ref, triu_ref, ex_ref, ovt_ref, eye_ref, o_ref, ks_ref, vs_ref, kw_ref, vw_ref, kc_ref,
                vc_ref):
    i = pl.program_id(1)

    @pl.when(i == 0)
    def _():
        rk, pvs, one = rk_ref[...], pvs_ref[...], one_ref[...]
        for src_ref, k_dst, v_dst in ((kvs_ref, ks_ref, vs_ref), (kvw_ref, kw_ref, vw_ref)):
            kv = src_ref[...]
            k_dst[...] = jnp.dot(kv, rk, preferred_element_type=F32).astype(BF16)
            v_dst[...] = (jnp.dot(kv, pvs, preferred_element_type=F32) + one).astype(BF16)
        kv = kvc_ref[0]
        kc_ref[...] = jnp.dot(kv, rk, preferred_element_type=F32).astype(BF16)
        vc_ref[...] = jnp.dot(kv, pvs, preferred_element_type=F32).astype(BF16)

    g = g_ref[...]
    q_stack = _stack_heads(q_ref[...], hm_ref, N_HEADS)
    n_cmp_rows = kc_ref.shape[0]
    head = lambda x, h: x[h * TQ:(h + 1) * TQ]

    t_c = i * TQ + lax.broadcasted_iota(jnp.int32, (TQ, n_cmp_rows), 0)
    c_idx = lax.broadcasted_iota(jnp.int32, (TQ, n_cmp_rows), 1)
    cmp_mask = c_idx * NSA_CMP_STRIDE + (NSA_CMP_LEN - 1) <= t_c
    s_cmp = _nt_dot(q_stack, kc_ref[...])
    p_sum = jnp.zeros((TQ, n_cmp_rows), F32)
    o_cmp = []
    for h in range(N_HEADS):
        s = jnp.where(cmp_mask, head(s_cmp, h), NEG_INF)
        m = jnp.max(s, axis=-1, keepdims=True)
        p = jnp.where(cmp_mask, jnp.exp(s - m), 0.0)
        p = p / jnp.maximum(jnp.sum(p, axis=-1, keepdims=True), 1e-30)
        p_sum = p_sum + p
        o_cmp.append(jnp.dot(p.astype(BF16), vc_ref[...], preferred_element_type=F32))

    imp = lax.dot_general(ovt_ref[...], p_sum, (((1,), (1,)), ((), ())), preferred_element_type=F32,
                          precision=lax.Precision.HIGHEST)[0:n_sel_blocks]
    t_r = i * TQ + lax.broadcasted_iota(jnp.int32, (n_sel_blocks, TQ), 1)
    j = lax.broadcasted_iota(jnp.int32, (n_sel_blocks, TQ), 0)
    t_blk = t_r >> 6
    forced = (j == 0) | (j == t_blk) | (j == t_blk - 1)
    imp = jnp.where(forced, NSA_FORCED_SCORE, imp)
    imp = jnp.where(j * NSA_SEL_LEN <= t_r, imp, NEG_INF)
    rank = jnp.zeros((n_sel_blocks, TQ), F32)
    for i2 in range(n_sel_blocks):
        row = imp[i2:i2 + 1, :]
        beats = jnp.where(row > imp, 1.0, jnp.where(row == imp, jnp.where(j > i2, 1.0, 0.0), 0.0))
        rank = rank + beats
    dropped_t = jnp.concatenate([jnp.where(rank < float(NSA_TOP_N), 0.0, -1.0),
                                 jnp.full((LANES - n_sel_blocks, TQ), -1.0, F32)], axis=0).astype(BF16)
    dropped = _nt_dot(eye_ref[...], dropped_t).astype(BF16)

    def sel_step(c, carry, bias=None):
        rows = _chunk_rows(c)
        pen = jnp.dot(dropped, ex_ref[c], preferred_element_type=F32)
        if bias is not None:
            pen = pen + bias
        s_all = _nt_dot(q_stack, ks_ref[rows, :])
        s = jnp.concatenate([head(s_all, h) + pen for h in range(N_HEADS)], axis=0)
        return _flash_update(carry, s, vs_ref[rows, :])

    sel = _chunk_loop(i, sel_step, _flash_init(N_HEADS * TQ))
    _, sel_acc = sel_step(i, sel, tril_ref[...])

    win = _flash_init(N_HEADS * TQ)
    for back, bias_ref in ((2, triu_ref), (1, None), (0, tril_ref)):
        c = jnp.maximum(i - back, 0)
        rows = _chunk_rows(c)
        void = jnp.where(i >= back, 0.0, NEG_INF)
        pen = void if bias_ref is None else bias_ref[...] + void
        s_all = _nt_dot(q_stack, kw_ref[rows, :])
        s = jnp.concatenate([head(s_all, h) + pen for h in range(N_HEADS)], axis=0)
        win = _flash_update(win, s, vw_ref[rows, :])
    _, win_acc = win

    o_sel = _flash_out(sel_acc)
    o_win = _flash_out(win_acc)
    outs = []
    for h in range(N_HEADS):
        outs.append(g[:, h:h + 1] * o_cmp[h] + g[:, 4 + h:5 + h] * head(o_sel, h)
                    + g[:, 8 + h:9 + h] * head(o_win, h))
    o_ref[...] = _place_heads(outs, pt_ref)


def _nsa(bm, kvcmp, d, head_mask, overlap_t, k, batch, seq):
    nq = seq // TQ
    nc = seq // TK
    n_cmp_rows = kvcmp.shape[1]
    const2 = lambda shape: pl.BlockSpec(shape, lambda b, i: (0, 0))
    const3 = lambda shape: pl.BlockSpec(shape, lambda b, i: (0, 0, 0))
    return pl.pallas_call(
        functools.partial(_nsa_kernel, seq // NSA_SEL_LEN),
        grid=(batch, nq),
        in_specs=[pl.BlockSpec((TQ, 256), lambda b, i: (b * nq + i, 0)),
                  pl.BlockSpec((seq, 128), lambda b, i: (b, 4)),
                  pl.BlockSpec((seq, 128), lambda b, i: (b, 5)),
                  pl.BlockSpec((1, n_cmp_rows, 128), lambda b, i: (b, 0, 0)),
                  pl.BlockSpec((TQ, 128), lambda b, i: (b * nq + i, 0)),
                  const2((N_HEADS, 256)), const2((128, 256)), const2((128, 128)), const2((1, 128)),
                  const3((4, 128, 256)), const2((TQ, TK)), const2((TQ, TK)), const3((nc, LANES, TK)),
                  const2((LANES, n_cmp_rows)), const2((256, 256))],
        out_specs=pl.BlockSpec((TQ, 256), lambda b, i: (b * nq + i, 0)),
        out_shape=jax.ShapeDtypeStruct((batch * seq, 256), BF16),
        scratch_shapes=[pltpu.VMEM((seq, 256), BF16), pltpu.VMEM((seq, 128), BF16),
                        pltpu.VMEM((seq, 256), BF16), pltpu.VMEM((seq, 128), BF16),
                        pltpu.VMEM((n_cmp_rows, 256), BF16), pltpu.VMEM((n_cmp_rows, 128), BF16)],
        compiler_params=_attn_params(),
        name="nsa_attn",
    )(bm, bm, bm, kvcmp, d, head_mask, k["rk"], k["pvs"], k["one_row"], k["pt4"], k["tril"], k["triu"],
      k["expand"], overlap_t, k["eye"])


def _dsa_kernel(n_keep, idx_bits, q_ref, kvd_ref, qi_ref, ki_ref, w_ref, hm_ref, rk_ref, pvs_ref,
                one_ref, pt_ref, tril_ref, trilt_ref, eye_ref, eyef_ref, bigeye_ref, o_ref, kd_ref, vd_ref, sc_ref,
                qit_ref):
    i = pl.program_id(1)

    @pl.when(i == 0)
    def _():
        kv = kvd_ref[...]
        kd_ref[...] = jnp.dot(kv, rk_ref[...], preferred_element_type=F32).astype(BF16)
        vd_ref[...] = (jnp.dot(kv, pvs_ref[...], preferred_element_type=F32) + one_ref[...]).astype(BF16)

    head = lambda x, h: x[h * TQ:(h + 1) * TQ]

    qi_t = _nt_dot(eye_ref[...], qi_ref[...])
    feat = lax.broadcasted_iota(jnp.int32, qi_t.shape, 0)
    for h in range(IDX_HEADS):
        qit_ref[:, h * TQ:(h + 1) * TQ] = jnp.where((feat >> 5) == h, qi_t, 0.0).astype(BF16)
    w_t = lax.dot_general(eyef_ref[...], w_ref[...], (((1,), (1,)), ((), ())),
                          preferred_element_type=F32, precision=lax.Precision.HIGHEST)
    ws = [w_t[h:h + 1, :] for h in range(IDX_HEADS)]

    def score_chunk(c, bias):
        logits = jnp.dot(ki_ref[_chunk_rows(c), :], qit_ref[...], preferred_element_type=F32)
        score = jnp.maximum(logits[:, 0:TQ], 0.0) * ws[0]
        for h in range(1, IDX_HEADS):
            score = score + jnp.maximum(logits[:, h * TQ:(h + 1) * TQ], 0.0) * ws[h]
        if bias is not None:
            score = score + bias
        sc_ref[c] = score

    def score_body(c, carry):
        score_chunk(c, None)
        return carry

    lax.fori_loop(0, i, score_body, 0)
    score_chunk(i, trilt_ref[...])

    def count(pred):
        def body(c, part):
            hit = jnp.where(pred(sc_ref[c], c), 1.0, 0.0)
            return part + jnp.sum(hit.reshape(TK // 8, 8, TQ), axis=0)
        part = lax.fori_loop(0, i + 1, body, jnp.zeros((8, TQ), F32))
        return jnp.sum(part, axis=0, keepdims=True)

    def key_to_float(key):
        bits = jnp.where(key < 0, key ^ jnp.int32(0x7FFFFFFF), key)
        return lax.bitcast_convert_type(bits, F32)

    keep = float(n_keep)
    total = ((i + 1) * TK).astype(F32)
    cnt0 = count(lambda s, c: s >= 0.0)
    thr0 = jnp.where(cnt0 >= keep, jnp.int32(0), jnp.int32(INT_MIN))
    cnt0 = jnp.where(cnt0 >= keep, cnt0, total)

    def thr_bit(it, carry):
        thr, cnt_thr = carry
        cand = thr | jnp.left_shift(jnp.int32(1), 30 - it)
        cand_f = key_to_float(cand)
        cnt = count(lambda s, c: s >= cand_f)
        take = cnt >= keep
        return jnp.where(take, cand, thr), jnp.where(take, cnt, cnt_thr)

    thr, cnt_thr = lax.fori_loop(0, 31, thr_bit, (thr0, cnt0))
    thr_f = key_to_float(thr)

    def key_index(c):
        return c * TK + lax.broadcasted_iota(jnp.int32, (TK, TQ), 0)

    def resolve_ties(_):
        need = keep - count(lambda s, c: s > thr_f)

        def cut_bit(it, cut):
            cand = cut | jnp.left_shift(jnp.int32(1), idx_bits - 1 - it)
            cnt = count(lambda s, c: (s == thr_f) & (key_index(c) < cand))
            return jnp.where(cnt < need, cand, cut)

        return lax.fori_loop(0, idx_bits, cut_bit, jnp.zeros((1, TQ), jnp.int32))

    has_excess = jnp.max(jnp.abs(cnt_thr - keep)) > 0.0
    cut = lax.cond(has_excess, resolve_ties, lambda _: jnp.full((1, TQ), 2 ** idx_bits, jnp.int32), 0)

    q_stack = _stack_heads(q_ref[...], hm_ref, N_HEADS)
    big_eye = bigeye_ref[...]

    def attn_step(c, carry, bias=None):
        rows = _chunk_rows(c)
        sc = sc_ref[c]
        tie = jnp.where(key_index(c) <= cut, 0.0, -1.0)
        dropped_t = jnp.where(sc > thr_f, 0.0, jnp.where(sc == thr_f, tie, -1.0)).astype(BF16)
        pen = _nt_dot(big_eye, dropped_t)
        if bias is not None:
            pen = pen + bias
        s_all = _nt_dot(q_stack, kd_ref[rows, :])
        s = jnp.concatenate([head(s_all, h) + pen for h in range(N_HEADS)], axis=0)
        return _flash_update(carry, s, vd_ref[rows, :])

    carry = _chunk_loop(i, attn_step, _flash_init(N_HEADS * TQ))
    _, acc = attn_step(i, carry, tril_ref[...])
    out = _flash_out(acc)
    o_ref[...] = _place_heads([head(out, h) for h in range(N_HEADS)], pt_ref)


def _dsa(bm, cidx, d, head_mask, k, batch, seq):
    nq = seq // TQ
    nc = seq // TK
    n_keep = min(DSA_TOP_K, seq // DSA_KEEP_DIV)
    assert n_keep <= TK
    idx_bits = int(np.log2(seq))
    assert 2 ** idx_bits == seq
    const2 = lambda shape: pl.BlockSpec(shape, lambda b, i: (0, 0))
    const3 = lambda shape: pl.BlockSpec(shape, lambda b, i: (0, 0, 0))
    return pl.pallas_call(
        functools.partial(_dsa_kernel, n_keep, idx_bits),
        grid=(batch, nq),
        in_specs=[pl.BlockSpec((TQ, 256), lambda b, i: (b * nq + i, 1)),
                  pl.BlockSpec((seq, 128), lambda b, i: (b, 6)),
                  pl.BlockSpec((TQ, 256), lambda b, i: (b * nq + i, 0)),
                  pl.BlockSpec((seq, 256), lambda b, i: (b, 1)),
                  pl.BlockSpec((TQ, 128), lambda b, i: (b * nq + i, 1)),
                  const2((N_HEADS, 256)), const2((128, 256)), const2((128, 128)),
                  const2((1, 128)), const3((4, 128, 256)), const2((TQ, TK)), const2((TK, TQ)),
                  const2((256, 256)), const2((128, 128)), const2((256, 256))],
        out_specs=pl.BlockSpec((TQ, 256), lambda b, i: (b * nq + i, 0)),
        out_shape=jax.ShapeDtypeStruct((batch * seq, 256), BF16),
        scratch_shapes=[pltpu.VMEM((seq, 256), BF16), pltpu.VMEM((seq, 128), BF16),
                        pltpu.VMEM((nc, TK, TQ), F32), pltpu.VMEM((256, IDX_HEADS * TQ), BF16)],
        compiler_params=_attn_params(),
        name="dsa_attn",
    )(bm, bm, cidx, cidx, d, head_mask, k["rk"], k["pvs"], k["one_row"], k["pt4"], k["tril"], k["trilt"],
      k["eye"], k["eyef"], k["big_eye"])


def _mix_kernel(x_ref, g1_ref, wg_ref, o0_ref, o1_ref, o2_ref, o3_ref, wb_ref, wo_ref, y_ref):
    x = x_ref[...]
    hb = _rms(x, g1_ref[...]).astype(BF16)
    mixed = jnp.zeros((TM, D_MODEL), F32)
    for n, o_ref in enumerate((o0_ref, o1_ref, o2_ref, o3_ref)):
        zg = jnp.dot(hb, wg_ref[:, n * D_MODEL:(n + 1) * D_MODEL], preferred_element_type=F32)
        gate = 1.0 / (1.0 + jnp.exp(-zg))
        lifted = jnp.dot(o_ref[...], wb_ref[n], preferred_element_type=F32)
        mixed = mixed + gate * lifted
    y_ref[...] = x + jnp.dot(mixed.astype(BF16), wo_ref[...], preferred_element_type=F32)


def _mix(x2, g1, wg, outs, wb, wo):
    t = x2.shape[0]
    rows = lambda w: pl.BlockSpec((TM, w), lambda i: (i, 0))
    return pl.pallas_call(
        _mix_kernel,
        grid=(t // TM,),
        in_specs=[rows(D_MODEL), pl.BlockSpec((1, D_MODEL), lambda i: (0, 0)),
                  pl.BlockSpec((D_MODEL, 4 * D_MODEL), lambda i: (0, 0)),
                  rows(256), rows(256), rows(256), rows(256),
                  pl.BlockSpec((4, 256, D_MODEL), lambda i: (0, 0, 0)),
                  pl.BlockSpec((D_MODEL, D_MODEL), lambda i: (0, 0))],
        out_specs=rows(D_MODEL),
        out_shape=jax.ShapeDtypeStruct((t, D_MODEL), F32),
        compiler_params=_row_params(),
        name="gate_mix",
    )(x2, g1, wg, *outs, wb, wo)


def _ffn_kernel(final, x_ref, g2_ref, wu_ref, wd_ref, gf_ref, y_ref):
    x = x_ref[...]
    hb = _rms(x, g2_ref[...]).astype(BF16)
    acc = jnp.zeros((TM, D_MODEL), F32)
    for jc in range(D_FF // D_MODEL):
        u = jnp.dot(hb, wu_ref[:, jc * D_MODEL:(jc + 1) * D_MODEL], preferred_element_type=F32)
        u = jnp.square(jnp.maximum(u, 0.0)).astype(BF16)
        acc = acc + jnp.dot(u, wd_ref[jc * D_MODEL:(jc + 1) * D_MODEL, :], preferred_element_type=F32)
    y = x + acc
    if final:
        y = _rms(y, gf_ref[...])
    y_ref[...] = y


def _ffn(x2, g2, wu, wd, gf, final):
    t = x2.shape[0]
    rows = pl.BlockSpec((TM, D_MODEL), lambda i: (i, 0))
    vec = pl.BlockSpec((1, D_MODEL), lambda i: (0, 0))
    return pl.pallas_call(
        functools.partial(_ffn_kernel, final),
        grid=(t // TM,),
        in_specs=[rows, vec, pl.BlockSpec((D_MODEL, D_FF), lambda i: (0, 0)),
                  pl.BlockSpec((D_FF, D_MODEL), lambda i: (0, 0)), vec],
        out_specs=rows,
        out_shape=jax.ShapeDtypeStruct((t, D_MODEL), F32),
        compiler_params=_row_params(),
        name="ffn",
    )(x2, g2, wu, wd, gf)


def _compress_weights(cmp_w, cmp_pe):
    wk = cmp_w[0].reshape(NSA_CMP_LEN, HEAD_DIM, HEAD_DIM)
    wv = cmp_w[1].reshape(NSA_CMP_LEN, HEAD_DIM, HEAD_DIM)
    zero = jnp.zeros_like(wk)
    full = jnp.concatenate([jnp.concatenate([wk, zero], axis=2), jnp.concatenate([zero, wv], axis=2)], axis=1)
    half = NSA_CMP_LEN // 2
    w_top = full[:half].reshape(half * 128, 128).astype(BF16)
    w_bot = full[half:].reshape(half * 128, 128).astype(BF16)
    pe = jnp.concatenate([cmp_pe[0], cmp_pe[1]], axis=1)
    pe2 = pe.reshape(2, half * 128)
    return w_top, w_bot, pe2


def _overlap_matrix(seq):
    n_rows = seq // NSA_CMP_STRIDE
    cs = np.arange(n_rows) * NSA_CMP_STRIDE
    sb = np.arange(LANES) * NSA_SEL_LEN
    ov = np.maximum(np.minimum(cs[:, None] + NSA_CMP_LEN, sb[None, :] + NSA_SEL_LEN)
                    - np.maximum(cs[:, None], sb[None, :]), 0).astype(np.float32) / NSA_CMP_LEN
    ov[:, seq // NSA_SEL_LEN:] = 0.0
    ov[(seq - NSA_CMP_LEN) // NSA_CMP_STRIDE + 1:, :] = 0.0
    return jnp.asarray(np.ascontiguousarray(ov.T))


def kernel(x, positions, norm1_g, w_in, mla_q_norm_g, mla_w_uq, mla_kv_norm_g, mla_w_ukv, nsa_cmp_pe, nsa_cmp_w,
           fox_f_bias, w_branch, w_out, norm2_g, w_up, w_down, final_g):
    batch, seq, _ = x.shape
    depth = w_in.shape[0]
    assert seq % TQ == 0 and seq % TM == 0 and seq // NSA_SEL_LEN <= LANES and TQ == TK
    assert seq // NSA_CMP_STRIDE <= LANES and (seq // NSA_CMP_STRIDE) % 8 == 0

    src, scale = _inproj_layout()
    q_src, k_src, v_src = _mla_up_layouts()
    cs3, place3 = _rope_tables(positions)
    head_mask = _head_masks(N_HEADS, 256)
    consts = _placement_constants(seq)
    eye = jnp.eye(128, dtype=F32)
    overlap_t = _overlap_matrix(seq)

    x2 = x.reshape(batch * seq, D_MODEL)
    for l in range(depth):
        w_packed = _pack_cols(w_in[l, :, :GATE_BASE], src, scale).astype(BF16)
        wg = w_in[l, :, GATE_BASE:].astype(BF16)
        wuq = _pack_cols(mla_w_uq[l], q_src, np.full(512, (MLA_NOPE + MLA_ROPE) ** -0.5, np.float32)).astype(BF16)
        wuk = _pack_cols(mla_w_ukv[l], k_src, np.ones(512, np.float32)).astype(BF16)
        wuv = _pack_cols(mla_w_ukv[l], v_src, np.ones(256, np.float32)).astype(BF16)
        fb = jnp.zeros((1, 128), F32).at[0, :N_HEADS].set(fox_f_bias[l])
        g1 = norm1_g[l][None, :]

        a, bm, kvc, cidx, d, e = _inproj(x2, g1, w_packed, cs3, place3, mla_q_norm_g[l][None, :], wuq,
                                          mla_kv_norm_g[l][None, :], wuk, wuv, fb, seq)
        w_top, w_bot, pe2 = _compress_weights(nsa_cmp_w[l], nsa_cmp_pe[l])
        kvcmp = _compress(kvc.reshape(batch, seq // NSA_CMP_STRIDE, NSA_CMP_STRIDE * 128), pe2, w_top, w_bot)

        o_mla = _mla(a, consts, batch, seq)
        o_nsa = _nsa(bm, kvcmp, d, head_mask, overlap_t, consts, batch, seq)
        o_fox = _fox(e, d, head_mask, eye, consts, batch, seq)
        o_dsa = _dsa(bm, cidx, d, head_mask, consts, batch, seq)

        x2 = _mix(x2, g1, wg, (o_mla, o_nsa, o_fox, o_dsa), w_branch[l].astype(BF16), w_out[l].astype(BF16))
        x2 = _ffn(x2, norm2_g[l][None, :], w_up[l].astype(BF16), w_down[l].astype(BF16), final_g[None, :],
                  l == depth - 1)
    return x2.reshape(batch, seq, D_MODEL)
```

```python
import functools

import numpy as np
import jax
import jax.numpy as jnp
from jax import lax
from jax.experimental import pallas as pl
from jax.experimental.pallas import tpu as pltpu

F32 = jnp.float32
BF16 = jnp.bfloat16

D_MODEL = 1024
ROPE_THETA = 500000.0
NEG_INF = -1e30
NORM_EPS = 1e-6
HEAD_DIM = 64
N_HEADS = 4
D_FF = 4 * D_MODEL
MLA_NOPE, MLA_ROPE, MLA_V = 64, 32, 64
MLA_Q_LORA, MLA_KV_LORA = 256, 128
NSA_CMP_LEN, NSA_CMP_STRIDE, NSA_SEL_LEN = 32, 16, 64
NSA_TOP_N, NSA_WINDOW, NSA_FORCED_SCORE = 8, 512, 1e4
IDX_HEADS, IDX_DIM = 8, 32
DSA_TOP_K, DSA_KEEP_DIV = 256, 4
MLA_COLS, NSA_COLS, FOX_COLS, DSA_COLS = 416, 652, 772, 680
GATE_BASE = MLA_COLS + NSA_COLS + FOX_COLS + DSA_COLS

LANES = 128
TQ = 256
TK = 256
TM = 256
VMEM_LIMIT = 56 * 1024 * 1024
INT_MIN = -2 ** 31
LOG2E = 1.4426950408889634

P_CQ, P_CKV, P_KR = 0, 256, 384
P_NQ, P_DQ = 512, 768
P_KVC, P_KVS, P_KVW, P_KVD = 1024, 1152, 1280, 1408
P_QI, P_KI = 1536, 1792
P_G, P_W = 2048, 2176
P_FQ, P_FK, P_FV, P_FF = 2304, 2560, 2816, 3072
P_TOTAL = 3200


def _inproj_layout():
    src = -np.ones(P_TOTAL, np.int64)
    scale = np.ones(P_TOTAL, np.float32)
    mla, nsa, fox, dsa = 0, MLA_COLS, MLA_COLS + NSA_COLS, MLA_COLS + NSA_COLS + FOX_COLS
    ar = np.arange
    src[P_CQ:P_CQ + 256] = mla + ar(256)
    src[P_CKV:P_CKV + 128] = mla + 256 + ar(128)
    src[P_KR + 64:P_KR + 96] = mla + 384 + ar(32)
    src[P_NQ:P_NQ + 256] = nsa + ar(256)
    scale[P_NQ:P_NQ + 256] = HEAD_DIM ** -0.5 * LOG2E
    src[P_DQ:P_DQ + 256] = dsa + ar(256)
    scale[P_DQ:P_DQ + 256] = HEAD_DIM ** -0.5 * LOG2E
    for base, k_off, v_off, origin in ((P_KVC, 256, 320, nsa), (P_KVS, 384, 448, nsa),
                                       (P_KVW, 512, 576, nsa), (P_KVD, 256, 320, dsa)):
        src[base:base + 64] = origin + k_off + ar(64)
        src[base + 64:base + 128] = origin + v_off + ar(64)
    src[P_QI:P_QI + 256] = dsa + 384 + ar(256)
    scale[P_QI:P_QI + 256] = (IDX_DIM * IDX_HEADS) ** -0.5
    for r in range(IDX_HEADS):
        src[P_KI + 32 * r:P_KI + 32 * r + 32] = dsa + 640 + ar(32)
    src[P_G:P_G + 12] = nsa + 640 + ar(12)
    src[P_W:P_W + 8] = dsa + 672 + ar(8)
    src[P_FQ:P_FQ + 256] = fox + ar(256)
    scale[P_FQ:P_FQ + 256] = HEAD_DIM ** -0.5 * LOG2E
    src[P_FK:P_FK + 256] = fox + 256 + ar(256)
    src[P_FV:P_FV + 256] = fox + 512 + ar(256)
    src[P_FF:P_FF + 4] = fox + 768 + ar(4)
    return src, scale


def _pack_cols(w, src, scale):
    pieces, start = [], 0
    for end in range(1, len(src) + 1):
        run_ends = (end == len(src) or scale[end] != scale[start]
                    or (src[end] != src[end - 1] + 1 if src[start] >= 0 else src[end] >= 0))
        if run_ends:
            if src[start] >= 0:
                pieces.append(w[:, int(src[start]):int(src[start]) + end - start] * float(scale[start]))
            else:
                pieces.append(jnp.zeros((w.shape[0], end - start), w.dtype))
            start = end
    return jnp.concatenate(pieces, axis=1)


def _mla_up_layouts():
    q_src = -np.ones(512, np.int64)
    k_src = -np.ones(512, np.int64)
    v_src = np.zeros(256, np.int64)
    for h in range(N_HEADS):
        q_src[h * 128:h * 128 + 96] = h * 96 + np.arange(96)
        k_src[h * 128:h * 128 + 64] = h * 128 + np.arange(64)
        v_src[h * 64:h * 64 + 64] = h * 128 + 64 + np.arange(64)
    return q_src, k_src, v_src


def _rope_tables(positions):
    pos = positions.astype(F32).reshape(-1)[None, :]
    n = pos.shape[1]
    rots = (("mla", MLA_ROPE), ("head", HEAD_DIM // 4), ("idx", IDX_DIM // 4))
    inv = jnp.concatenate([ROPE_THETA ** (-jnp.arange(0, rot, 2, dtype=F32) / rot) for _, rot in rots])
    ang = inv[:, None] * pos
    cos_all, sin_all = jnp.cos(ang), jnp.sin(ang)
    parts, offs = [], {}
    col = row = 0
    for name, rot in rots:
        parts += [cos_all[row:row + rot // 2], sin_all[row:row + rot // 2]]
        offs[name] = (col, col + rot // 2, rot // 2)
        col += rot
        row += rot // 2
    one_col = col
    parts.append(jnp.ones((64 - col, n), F32))
    cs = jnp.concatenate(parts, axis=0)
    hi = cs.astype(BF16)
    rest = cs - hi.astype(F32)
    mid = rest.astype(BF16)
    low = (rest - mid.astype(F32)).astype(BF16)
    cs3 = jnp.concatenate([hi, mid, low, jnp.zeros((64, n), BF16)], axis=0).T

    place = np.zeros((64, 768), np.float32)

    def fill(cos_base, sin_base, name, group, first, n_groups):
        c0, s0, half = offs[name]
        place[one_col, cos_base:cos_base + 128] = 1.0
        for g in range(n_groups):
            lo = g * group + first
            for j in range(half):
                for lane, sign in ((lo + j, -1.0), (lo + half + j, 1.0)):
                    place[one_col, cos_base + lane] = 0.0
                    place[c0 + j, cos_base + lane] = 1.0
                    place[s0 + j, sin_base + lane] = sign

    fill(0, 128, "mla", 128, 64, 1)
    fill(256, 384, "head", 64, 0, 2)
    fill(512, 640, "idx", 32, 0, 4)
    place3 = np.concatenate([place, place, place, np.zeros((64, 768), np.float32)], axis=0)
    return cs3, jnp.asarray(place3, BF16)


def _head_masks(n_heads, width):
    lane = np.arange(width)
    return jnp.asarray((lane[None, :] // (width // n_heads) == np.arange(n_heads)[:, None]), BF16)


def _placement_constants(seq):
    d = np.arange(64)
    rk = np.zeros((128, 256), np.float32)
    pvs = np.zeros((128, 128), np.float32)
    pv4 = np.zeros((4, 256, 128), np.float32)
    pt4 = np.zeros((4, 128, 256), np.float32)
    pvs[64 + d, d] = 1.0
    for h in range(N_HEADS):
        rk[d, h * 64 + d] = 1.0
        pv4[h, h * 64 + d, d] = 1.0
        pt4[h, d, h * 64 + d] = 1.0
    one_row = np.zeros((1, 128), np.float32)
    one_row[0, 64] = 1.0
    r, c = np.arange(TQ)[:, None], np.arange(TK)[None, :]
    tril = np.where(c <= r, 0.0, NEG_INF).astype(np.float32)
    triu = np.where(c > r, 0.0, NEG_INF).astype(np.float32)
    nc = seq // TK
    key_blk = (np.arange(seq) // NSA_SEL_LEN).reshape(nc, 1, TK)
    expand = (key_blk == np.arange(LANES).reshape(1, LANES, 1)).astype(np.float32)
    bf = lambda a: jnp.asarray(a, BF16)
    return dict(rk=bf(rk), pvs=bf(pvs), pv4=bf(pv4), pt4=bf(pt4), one_row=jnp.asarray(one_row),
                tril=jnp.asarray(tril), triu=jnp.asarray(triu), expand=bf(expand * -NEG_INF),
                eye=bf(np.eye(256, dtype=np.float32)), big_eye=bf(np.eye(256, dtype=np.float32) * -NEG_INF), eyef=jnp.asarray(np.eye(128, dtype=np.float32)),
                trilt=jnp.asarray(np.ascontiguousarray(tril.T)))


def _rms(x, g):
    return x * lax.rsqrt(jnp.mean(x * x, axis=-1, keepdims=True) + NORM_EPS) * g


def _rope_tile(x, cos, sin, is_x1, half):
    fwd = pltpu.roll(x, LANES - half, axis=1)
    bwd = pltpu.roll(x, half, axis=1)
    return x * cos + jnp.where(is_x1, fwd, bwd) * sin


def _nt_dot(a, b):
    return lax.dot_general(a, b, (((1,), (1,)), ((), ())), preferred_element_type=F32)


def _attn_params():
    return pltpu.CompilerParams(dimension_semantics=("arbitrary", "arbitrary"), vmem_limit_bytes=VMEM_LIMIT)


def _row_params():
    return pltpu.CompilerParams(dimension_semantics=("arbitrary",), vmem_limit_bytes=VMEM_LIMIT)


def _inproj_kernel(tiles_per_seq, x_ref, g1_ref, w_ref, cs_ref, place_ref, gq_ref, wuq_ref, gkv_ref, wuk_ref,
                   wuv_ref, fb_ref, a_ref, b_ref, kvc_ref, c_ref, d_ref, e_ref, carry_ref):
    i = pl.program_id(0)
    x = x_ref[...]
    hb = _rms(x, g1_ref[...]).astype(BF16)
    z = jnp.dot(hb, w_ref[...], preferred_element_type=F32)

    lane = lax.broadcasted_iota(jnp.int32, (TM, LANES), 1)
    tab = jnp.dot(cs_ref[...], place_ref[...], preferred_element_type=F32)
    cos1, sin1 = tab[:, 0:128], tab[:, 128:256]
    cos3, sin3 = tab[:, 256:384], tab[:, 384:512]
    cos5, sin5 = tab[:, 512:640], tab[:, 640:768]
    x1_mla = (lane >= 64) & (lane < 80)
    x1_head = (lane & 63) < 8
    x1_idx = (lane & 31) < 4
    k_half = lane < 64
    cos4 = jnp.where(k_half, cos3, 1.0)
    sin4 = jnp.where(k_half, sin3, 0.0)

    cqn = _rms(z[:, P_CQ:P_CQ + 256], gq_ref[...]).astype(BF16)
    q_up = jnp.dot(cqn, wuq_ref[...], preferred_element_type=F32)
    ckvn = _rms(z[:, P_CKV:P_CKV + 128], gkv_ref[...]).astype(BF16)
    k_up = jnp.dot(ckvn, wuk_ref[...], preferred_element_type=F32)
    v_up = jnp.dot(ckvn, wuv_ref[...], preferred_element_type=F32)
    k_rot = _rope_tile(z[:, P_KR:P_KR + 128], cos1, sin1, x1_mla, 16)
    for h in range(N_HEADS):
        sl = slice(h * 128, (h + 1) * 128)
        a_ref[:, sl] = _rope_tile(q_up[:, sl], cos1, sin1, x1_mla, 16).astype(BF16)
        a_ref[:, 512 + h * 128:512 + (h + 1) * 128] = (k_up[:, sl] + k_rot).astype(BF16)
    a_ref[:, 1024:1280] = v_up.astype(BF16)

    for t in range(4):
        b_ref[:, t * 128:(t + 1) * 128] = _rope_tile(
            z[:, P_NQ + t * 128:P_NQ + (t + 1) * 128], cos3, sin3, x1_head, 8).astype(BF16)
    kvc_ref[...] = _rope_tile(z[:, P_KVC:P_KVC + 128], cos4, sin4, x1_head & k_half, 8).astype(BF16)
    for t, off in enumerate((P_KVS, P_KVW, P_KVD)):
        b_ref[:, 512 + t * 128:512 + (t + 1) * 128] = _rope_tile(
            z[:, off:off + 128], cos4, sin4, x1_head & k_half, 8).astype(BF16)

    for t in range(4):
        c_ref[:, t * 128:(t + 1) * 128] = _rope_tile(
            z[:, P_QI + t * 128:P_QI + (t + 1) * 128], cos5, sin5, x1_idx, 4).astype(BF16)

    d_ref[:, 0:128] = 1.0 / (1.0 + jnp.exp(-z[:, P_G:P_G + 128]))
    d_ref[:, 128:256] = z[:, P_W:P_W + 128]
    f = z[:, P_FF:P_FF + 128] + fb_ref[...]
    log_f = jnp.minimum(f, 0.0) - jnp.log(1.0 + jnp.exp(-jnp.abs(f)))

    @pl.when(i % tiles_per_seq == 0)
    def _():
        carry_ref[...] = jnp.zeros_like(carry_ref)

    row = lax.broadcasted_iota(jnp.int32, (TM, TM), 0)
    col = lax.broadcasted_iota(jnp.int32, (TM, TM), 1)
    tri = jnp.where(col <= row, 1.0, 0.0).astype(F32)
    c = jnp.dot(tri, log_f, preferred_element_type=F32, precision=lax.Precision.HIGHEST)
    c = c + carry_ref[0:1, :]
    d_ref[:, 256:384] = c
    carry_ref[...] = jnp.broadcast_to(c[TM - 1:TM, :], carry_ref.shape)

    e_ref[...] = z[:, P_FQ:P_FQ + 768].astype(BF16)


def _inproj(x2, g1, w_packed, cs3, place3, gq, wuq, gkv, wuk, wuv, fb, seq):
    t = x2.shape[0]
    full = lambda shape: pl.BlockSpec(shape, lambda i: (0, 0))
    rows = lambda w: pl.BlockSpec((TM, w), lambda i: (i, 0))
    return pl.pallas_call(
        functools.partial(_inproj_kernel, seq // TM),
        grid=(t // TM,),
        in_specs=[rows(D_MODEL), full((1, D_MODEL)), full((D_MODEL, P_TOTAL)), rows(256), full((256, 768)),
                  full((1, 256)), full((256, 512)), full((1, 128)), full((128, 512)), full((128, 256)),
                  full((1, 128))],
        out_specs=[rows(1280), rows(896), rows(128), rows(512), rows(384), rows(768)],
        out_shape=[jax.ShapeDtypeStruct((t, 1280), BF16), jax.ShapeDtypeStruct((t, 896), BF16),
                   jax.ShapeDtypeStruct((t, 128), BF16), jax.ShapeDtypeStruct((t, 512), BF16),
                   jax.ShapeDtypeStruct((t, 384), F32), jax.ShapeDtypeStruct((t, 768), BF16)],
        scratch_shapes=[pltpu.VMEM((8, 128), F32)],
        compiler_params=_row_params(),
        name="inproj",
    )(x2, g1, w_packed, cs3, place3, gq, wuq, gkv, wuk, wuv, fb)


def _compress_kernel(r_ref, pe_ref, wt_ref, wb_ref, o_ref):
    r = r_ref[0].astype(F32)
    top = jnp.dot((r + pe_ref[0:1, :]).astype(BF16), wt_ref[...], preferred_element_type=F32)
    bot = jnp.dot((r + pe_ref[1:2, :]).astype(BF16), wb_ref[...], preferred_element_type=F32)
    n = bot.shape[0]
    o_ref[0] = (top + pltpu.roll(bot, n - 1, axis=0)).astype(BF16)


def _compress(kvc3, pe2, w_top, w_bot):
    b, n, _ = kvc3.shape
    return pl.pallas_call(
        _compress_kernel,
        grid=(b,),
        in_specs=[pl.BlockSpec((1, n, 2048), lambda i: (i, 0, 0)), pl.BlockSpec((2, 2048), lambda i: (0, 0)),
                  pl.BlockSpec((2048, 128), lambda i: (0, 0)), pl.BlockSpec((2048, 128), lambda i: (0, 0))],
        out_specs=pl.BlockSpec((1, n, 128), lambda i: (i, 0, 0)),
        out_shape=jax.ShapeDtypeStruct((b, n, 128), BF16),
        compiler_params=_row_params(),
        name="nsa_compress",
    )(kvc3, pe2, w_top, w_bot)


def _flash_init(rows):
    return jnp.full((rows, 1), NEG_INF, F32), jnp.zeros((rows, LANES), F32)


def _flash_update(carry, s, v):
    m, acc = carry
    m_new = jnp.maximum(m, jnp.max(s, axis=-1, keepdims=True))
    p = jnp.exp2(s - m_new).astype(BF16)
    alpha = jnp.exp2(m - m_new)
    return m_new, alpha * acc + jnp.dot(p, v, preferred_element_type=F32)


def _flash_update_many(carries, scores, values):
    ms = [jnp.maximum(m, jnp.max(s, axis=-1, keepdims=True)) for (m, _), s in zip(carries, scores)]
    ps = [jnp.exp2(s - m).astype(BF16) for s, m in zip(scores, ms)]
    alphas = [jnp.exp2(m_old - m) for (m_old, _), m in zip(carries, ms)]
    pvs = [jnp.dot(p, v, preferred_element_type=F32) for p, v in zip(ps, values)]
    return tuple((m, a * acc + pv) for m, a, (_, acc), pv in zip(ms, alphas, carries, pvs))


def _chunk_loop(n, step, init):
    carry = lax.fori_loop(0, n // 2, lambda c2, cr: step(2 * c2 + 1, step(2 * c2, cr)), init)
    return lax.fori_loop(2 * (n // 2), n, step, carry)


def _flash_out(acc):
    return acc / jnp.maximum(acc[:, 64:65], 1e-30)


def _place_heads(outs, pt_ref):
    res = None
    for h, o in enumerate(outs):
        t = jnp.dot(o.astype(BF16), pt_ref[h], preferred_element_type=F32)
        res = t if res is None else res + t
    return res.astype(BF16)


def _chunk_rows(c):
    return pl.ds(pl.multiple_of(c * TK, TK), TK)


def _stack_heads(x, mask_ref, n):
    return jnp.concatenate([x * mask_ref[h:h + 1, :] for h in range(n)], axis=0)


def _mla_kernel(q_ref, k_ref, v_ref, pv_ref, one_ref, pt_ref, tril_ref, o_ref, vaug_ref):
    i = pl.program_id(1)

    @pl.when(i == 0)
    def _():
        v = v_ref[...]
        for h in range(N_HEADS):
            vaug_ref[h] = (jnp.dot(v, pv_ref[h], preferred_element_type=F32) + one_ref[...]).astype(BF16)

    qs = [q_ref[:, h * 128:(h + 1) * 128] for h in range(N_HEADS)]

    def step(c, carries, bias=None):
        rows = _chunk_rows(c)
        scores = [_nt_dot(qs[h], k_ref[rows, h * 128:(h + 1) * 128]) for h in range(N_HEADS)]
        if bias is not None:
            scores = [s + bias for s in scores]
        return _flash_update_many(carries, scores, [vaug_ref[h, rows, :] for h in range(N_HEADS)])

    carries = _chunk_loop(i, step, tuple(_flash_init(TQ) for _ in range(N_HEADS)))
    carries = step(i, carries, tril_ref[...])
    o_ref[...] = _place_heads([_flash_out(acc) for _, acc in carries], pt_ref)


def _mla(a, k, batch, seq):
    nq = seq // TQ
    const2 = lambda shape: pl.BlockSpec(shape, lambda b, i: (0, 0))
    const3 = lambda shape: pl.BlockSpec(shape, lambda b, i: (0, 0, 0))
    return pl.pallas_call(
        _mla_kernel,
        grid=(batch, nq),
        in_specs=[pl.BlockSpec((TQ, 512), lambda b, i: (b * nq + i, 0)),
                  pl.BlockSpec((seq, 512), lambda b, i: (b, 1)),
                  pl.BlockSpec((seq, 256), lambda b, i: (b, 4)),
                  const3((4, 256, 128)), const2((1, 128)), const3((4, 128, 256)), const2((TQ, TK))],
        out_specs=pl.BlockSpec((TQ, 256), lambda b, i: (b * nq + i, 0)),
        out_shape=jax.ShapeDtypeStruct((batch * seq, 256), BF16),
        scratch_shapes=[pltpu.VMEM((N_HEADS, seq, 128), BF16)],
        compiler_params=_attn_params(),
        name="mla_attn",
    )(a, a, a, k["pv4"], k["one_row"], k["pt4"], k["tril"])


def _fox_kernel(n_chunks, q_ref, k_ref, v_ref, call_ref, hm_ref, eye_ref, pv_ref, one_ref, pt_ref,
                tril_ref, o_ref, vaug_ref, ct_ref):
    i = pl.program_id(1)

    @pl.when(i == 0)
    def _():
        v = v_ref[...]
        for h in range(N_HEADS):
            vaug_ref[h] = (jnp.dot(v, pv_ref[h], preferred_element_type=F32) + one_ref[...]).astype(BF16)
        ct = lax.dot_general(eye_ref[...], call_ref[...], (((1,), (1,)), ((), ())),
                             preferred_element_type=F32, precision=lax.Precision.HIGHEST)
        for c in range(n_chunks):
            ct_ref[c] = ct[0:8, c * TK:(c + 1) * TK] * LOG2E

    q_all = q_ref[...]
    qs = [q_all * hm_ref[h:h + 1, :] for h in range(N_HEADS)]

    def step(c, carries, bias=None):
        rows = _chunk_rows(c)
        k = k_ref[rows, :]
        cs = ct_ref[c]
        scores = [_nt_dot(qs[h], k) - cs[h:h + 1, :] for h in range(N_HEADS)]
        if bias is not None:
            scores = [s + bias for s in scores]
        return _flash_update_many(carries, scores, [vaug_ref[h, rows, :] for h in range(N_HEADS)])

    carries = _chunk_loop(i, step, tuple(_flash_init(TQ) for _ in range(N_HEADS)))
    carries = step(i, carries, tril_ref[...])
    o_ref[...] = _place_heads([_flash_out(acc) for _, acc in carries], pt_ref)


def _fox(e, d, head_mask, eye, k, batch, seq):
    nq = seq // TQ
    nc = seq // TK
    const2 = lambda shape: pl.BlockSpec(shape, lambda b, i: (0, 0))
    const3 = lambda shape: pl.BlockSpec(shape, lambda b, i: (0, 0, 0))
    return pl.pallas_call(
        functools.partial(_fox_kernel, nc),
        grid=(batch, nq),
        in_specs=[pl.BlockSpec((TQ, 256), lambda b, i: (b * nq + i, 0)),
                  pl.BlockSpec((seq, 256), lambda b, i: (b, 1)),
                  pl.BlockSpec((seq, 256), lambda b, i: (b, 2)),
                  pl.BlockSpec((seq, 128), lambda b, i: (b, 2)),
                  const2((N_HEADS, 256)), const2((128, 128)),
                  const3((4, 256, 128)), const2((1, 128)), const3((4, 128, 256)), const2((TQ, TK))],
        out_specs=pl.BlockSpec((TQ, 256), lambda b, i: (b * nq + i, 0)),
        out_shape=jax.ShapeDtypeStruct((batch * seq, 256), BF16),
        scratch_shapes=[pltpu.VMEM((N_HEADS, seq, 128), BF16), pltpu.VMEM((nc, 8, TK), F32)],
        compiler_params=_attn_params(),
        name="fox_attn",
    )(e, e, e, d, head_mask, eye, k["pv4"], k["one_row"], k["pt4"], k["tril"])


def _nsa_kernel(n_sel_blocks, q_ref, kvs_ref, kvw_ref, kvc_ref, g_ref, hm_ref, rk_ref, pvs_ref, one_ref,
                pt_ref, tril_ref, triu_ref, ex_ref, ovt_ref, eye_ref, o_ref, ks_ref, vs_ref, kw_ref, vw_ref, kc_ref,
                vc_ref):
    i = pl.program_id(1)

    @pl.when(i == 0)
    def _():
        rk, pvs, one = rk_ref[...], pvs_ref[...], one_ref[...]
        for src_ref, k_dst, v_dst in ((kvs_ref, ks_ref, vs_ref), (kvw_ref, kw_ref, vw_ref)):
            kv = src_ref[...]
            k_dst[...] = jnp.dot(kv, rk, preferred_element_type=F32).astype(BF16)
            v_dst[...] = (jnp.dot(kv, pvs, preferred_element_type=F32) + one).astype(BF16)
        kv = kvc_ref[0]
        kc_ref[...] = jnp.dot(kv, rk, preferred_element_type=F32).astype(BF16)
        vc_ref[...] = jnp.dot(kv, pvs, preferred_element_type=F32).astype(BF16)

    g = g_ref[...]
    q_stack = _stack_heads(q_ref[...], hm_ref, N_HEADS)
    n_cmp_rows = kc_ref.shape[0]
    head = lambda x, h: x[h * TQ:(h + 1) * TQ]

    t_c = i * TQ + lax.broadcasted_iota(jnp.int32, (TQ, n_cmp_rows), 0)
    c_idx = lax.broadcasted_iota(jnp.int32, (TQ, n_cmp_rows), 1)
    cmp_mask = c_idx * NSA_CMP_STRIDE + (NSA_CMP_LEN - 1) <= t_c
    s_cmp = _nt_dot(q_stack, kc_ref[...])
    p_sum = jnp.zeros((TQ, n_cmp_rows), F32)
    o_cmp = []
    for h in range(N_HEADS):
        s = jnp.where(cmp_mask, head(s_cmp, h), NEG_INF)
        m = jnp.max(s, axis=-1, keepdims=True)
        p = jnp.where(cmp_mask, jnp.exp2(s - m), 0.0)
        p = p / jnp.maximum(jnp.sum(p, axis=-1, keepdims=True), 1e-30)
        p_sum = p_sum + p
        o_cmp.append(jnp.dot(p.astype(BF16), vc_ref[...], preferred_element_type=F32))

    imp = lax.dot_general(ovt_ref[...], p_sum, (((1,), (1,)), ((), ())), preferred_element_type=F32,
                          precision=lax.Precision.HIGHEST)[0:n_sel_blocks]
    t_r = i * TQ + lax.broadcasted_iota(jnp.int32, (n_sel_blocks, TQ), 1)
    j = lax.broadcasted_iota(jnp.int32, (n_sel_blocks, TQ), 0)
    t_blk = t_r >> 6
    forced = (j == 0) | (j == t_blk) | (j == t_blk - 1)
    imp = jnp.where(forced, NSA_FORCED_SCORE, imp)
    imp = jnp.where(j * NSA_SEL_LEN <= t_r, imp, NEG_INF)
    rank = jnp.zeros((n_sel_blocks, TQ), F32)
    for i2 in range(n_sel_blocks):
        row = imp[i2:i2 + 1, :]
        beats = jnp.where(row > imp, 1.0, jnp.where(row == imp, jnp.where(j > i2, 1.0, 0.0), 0.0))
        rank = rank + beats
    dropped_t = jnp.concatenate([jnp.where(rank < float(NSA_TOP_N), 0.0, -1.0),
                                 jnp.full((LANES - n_sel_blocks, TQ), -1.0, F32)], axis=0).astype(BF16)
    dropped = _nt_dot(eye_ref[...], dropped_t).astype(BF16)

    def sel_step(c, carry, bias=None):
        rows = _chunk_rows(c)
        pen = jnp.dot(dropped, ex_ref[c], preferred_element_type=F32)
        if bias is not None:
            pen = pen + bias
        s_all = _nt_dot(q_stack, ks_ref[rows, :])
        s = jnp.concatenate([head(s_all, h) + pen for h in range(N_HEADS)], axis=0)
        return _flash_update(carry, s, vs_ref[rows, :])

    sel = _chunk_loop(i, sel_step, _flash_init(N_HEADS * TQ))
    _, sel_acc = sel_step(i, sel, tril_ref[...])

    win = _flash_init(N_HEADS * TQ)
    for back, bias_ref in ((2, triu_ref), (1, None), (0, tril_ref)):
        c = jnp.maximum(i - back, 0)
        rows = _chunk_rows(c)
        void = jnp.where(i >= back, 0.0, NEG_INF)
        pen = void if bias_ref is None else bias_ref[...] + void
        s_all = _nt_dot(q_stack, kw_ref[rows, :])
        s = jnp.concatenate([head(s_all, h) + pen for h in range(N_HEADS)], axis=0)
        win = _flash_update(win, s, vw_ref[rows, :])
    _, win_acc = win

    o_sel = _flash_out(sel_acc)
    o_win = _flash_out(win_acc)
    outs = []
    for h in range(N_HEADS):
        outs.append(g[:, h:h + 1] * o_cmp[h] + g[:, 4 + h:5 + h] * head(o_sel, h)
                    + g[:, 8 + h:9 + h] * head(o_win, h))
    o_ref[...] = _place_heads(outs, pt_ref)


def _nsa(bm, kvcmp, d, head_mask, overlap_t, k, batch, seq):
    nq = seq // TQ
    nc = seq // TK
    n_cmp_rows = kvcmp.shape[1]
    const2 = lambda shape: pl.BlockSpec(shape, lambda b, i: (0, 0))
    const3 = lambda shape: pl.BlockSpec(shape, lambda b, i: (0, 0, 0))
    return pl.pallas_call(
        functools.partial(_nsa_kernel, seq // NSA_SEL_LEN),
        grid=(batch, nq),
        in_specs=[pl.BlockSpec((TQ, 256), lambda b, i: (b * nq + i, 0)),
                  pl.BlockSpec((seq, 128), lambda b, i: (b, 4)),
                  pl.BlockSpec((seq, 128), lambda b, i: (b, 5)),
                  pl.BlockSpec((1, n_cmp_rows, 128), lambda b, i: (b, 0, 0)),
                  pl.BlockSpec((TQ, 128), lambda b, i: (b * nq + i, 0)),
                  const2((N_HEADS, 256)), const2((128, 256)), const2((128, 128)), const2((1, 128)),
                  const3((4, 128, 256)), const2((TQ, TK)), const2((TQ, TK)), const3((nc, LANES, TK)),
                  const2((LANES, n_cmp_rows)), const2((256, 256))],
        out_specs=pl.BlockSpec((TQ, 256), lambda b, i: (b * nq + i, 0)),
        out_shape=jax.ShapeDtypeStruct((batch * seq, 256), BF16),
        scratch_shapes=[pltpu.VMEM((seq, 256), BF16), pltpu.VMEM((seq, 128), BF16),
                        pltpu.VMEM((seq, 256), BF16), pltpu.VMEM((seq, 128), BF16),
                        pltpu.VMEM((n_cmp_rows, 256), BF16), pltpu.VMEM((n_cmp_rows, 128), BF16)],
        compiler_params=_attn_params(),
        name="nsa_attn",
    )(bm, bm, bm, kvcmp, d, head_mask, k["rk"], k["pvs"], k["one_row"], k["pt4"], k["tril"], k["triu"],
      k["expand"], overlap_t, k["eye"])


def _dsa_kernel(n_keep, idx_bits, q_ref, kvd_ref, qi_ref, ki_ref, w_ref, hm_ref, rk_ref, pvs_ref,
                one_ref, pt_ref, tril_ref, trilt_ref, eye_ref, eyef_ref, bigeye_ref, o_ref, kd_ref, vd_ref, sc_ref,
                qit_ref):
    i = pl.program_id(1)

    @pl.when(i == 0)
    def _():
        kv = kvd_ref[...]
        kd_ref[...] = jnp.dot(kv, rk_ref[...], preferred_element_type=F32).astype(BF16)
        vd_ref[...] = (jnp.dot(kv, pvs_ref[...], preferred_element_type=F32) + one_ref[...]).astype(BF16)

    head = lambda x, h: x[h * TQ:(h + 1) * TQ]

    qi_t = _nt_dot(eye_ref[...], qi_ref[...])
    feat = lax.broadcasted_iota(jnp.int32, qi_t.shape, 0)
    for h in range(IDX_HEADS):
        qit_ref[:, h * TQ:(h + 1) * TQ] = jnp.where((feat >> 5) == h, qi_t, 0.0).astype(BF16)
    w_t = lax.dot_general(eyef_ref[...], w_ref[...], (((1,), (1,)), ((), ())),
                          preferred_element_type=F32, precision=lax.Precision.HIGHEST)
    ws = [w_t[h:h + 1, :] for h in range(IDX_HEADS)]

    def score_chunk(c, bias):
        logits = jnp.dot(ki_ref[_chunk_rows(c), :], qit_ref[...], preferred_element_type=F32)
        score = jnp.maximum(logits[:, 0:TQ], 0.0) * ws[0]
        for h in range(1, IDX_HEADS):
            score = score + jnp.maximum(logits[:, h * TQ:(h + 1) * TQ], 0.0) * ws[h]
        if bias is not None:
            score = score + bias
        sc_ref[c] = score

    def score_body(c, carry):
        score_chunk(c, None)
        return carry

    lax.fori_loop(0, i, score_body, 0)
    score_chunk(i, trilt_ref[...])

    def count(pred):
        def body(c, part):
            hit = jnp.where(pred(sc_ref[c], c), 1.0, 0.0)
            return part + jnp.sum(hit.reshape(TK // 8, 8, TQ), axis=0)
        part = lax.fori_loop(0, i + 1, body, jnp.zeros((8, TQ), F32))
        return jnp.sum(part, axis=0, keepdims=True)

    def key_to_float(key):
        bits = jnp.where(key < 0, key ^ jnp.int32(0x7FFFFFFF), key)
        return lax.bitcast_convert_type(bits, F32)

    keep = float(n_keep)
    total = ((i + 1) * TK).astype(F32)
    cnt0 = count(lambda s, c: s >= 0.0)
    thr0 = jnp.where(cnt0 >= keep, jnp.int32(0), jnp.int32(INT_MIN))
    cnt0 = jnp.where(cnt0 >= keep, cnt0, total)

    def thr_bit(it, carry):
        thr, cnt_thr = carry
        cand = thr | jnp.left_shift(jnp.int32(1), 30 - it)
        cand_f = key_to_float(cand)
        cnt = count(lambda s, c: s >= cand_f)
        take = cnt >= keep
        return jnp.where(take, cand, thr), jnp.where(take, cnt, cnt_thr)

    thr, cnt_thr = lax.fori_loop(0, 31, thr_bit, (thr0, cnt0))
    thr_f = key_to_float(thr)

    def key_index(c):
        return c * TK + lax.broadcasted_iota(jnp.int32, (TK, TQ), 0)

    def resolve_ties(_):
        need = keep - count(lambda s, c: s > thr_f)

        def cut_bit(it, cut):
            cand = cut | jnp.left_shift(jnp.int32(1), idx_bits - 1 - it)
            cnt = count(lambda s, c: (s == thr_f) & (key_index(c) < cand))
            return jnp.where(cnt < need, cand, cut)

        return lax.fori_loop(0, idx_bits, cut_bit, jnp.zeros((1, TQ), jnp.int32))

    has_excess = jnp.max(jnp.abs(cnt_thr - keep)) > 0.0
    cut = lax.cond(has_excess, resolve_ties, lambda _: jnp.full((1, TQ), 2 ** idx_bits, jnp.int32), 0)

    q_stack = _stack_heads(q_ref[...], hm_ref, N_HEADS)
    big_eye = bigeye_ref[...]

    def attn_step(c, carry, bias=None):
        rows = _chunk_rows(c)
        sc = sc_ref[c]
        tie = jnp.where(key_index(c) <= cut, 0.0, -1.0)
        dropped_t = jnp.where(sc > thr_f, 0.0, jnp.where(sc == thr_f, tie, -1.0)).astype(BF16)
        pen = _nt_dot(big_eye, dropped_t)
        if bias is not None:
            pen = pen + bias
        s_all = _nt_dot(q_stack, kd_ref[rows, :])
        s = jnp.concatenate([head(s_all, h) + pen for h in range(N_HEADS)], axis=0)
        return _flash_update(carry, s, vd_ref[rows, :])

    carry = _chunk_loop(i, attn_step, _flash_init(N_HEADS * TQ))
    _, acc = attn_step(i, carry, tril_ref[...])
    out = _flash_out(acc)
    o_ref[...] = _place_heads([head(out, h) for h in range(N_HEADS)], pt_ref)


def _dsa(bm, cidx, d, head_mask, k, batch, seq):
    nq = seq // TQ
    nc = seq // TK
    n_keep = min(DSA_TOP_K, seq // DSA_KEEP_DIV)
    assert n_keep <= TK
    idx_bits = int(np.log2(seq))
    assert 2 ** idx_bits == seq
    const2 = lambda shape: pl.BlockSpec(shape, lambda b, i: (0, 0))
    const3 = lambda shape: pl.BlockSpec(shape, lambda b, i: (0, 0, 0))
    return pl.pallas_call(
        functools.partial(_dsa_kernel, n_keep, idx_bits),
        grid=(batch, nq),
        in_specs=[pl.BlockSpec((TQ, 256), lambda b, i: (b * nq + i, 1)),
                  pl.BlockSpec((seq, 128), lambda b, i: (b, 6)),
                  pl.BlockSpec((TQ, 256), lambda b, i: (b * nq + i, 0)),
                  pl.BlockSpec((seq, 256), lambda b, i: (b, 1)),
                  pl.BlockSpec((TQ, 128), lambda b, i: (b * nq + i, 1)),
                  const2((N_HEADS, 256)), const2((128, 256)), const2((128, 128)),
                  const2((1, 128)), const3((4, 128, 256)), const2((TQ, TK)), const2((TK, TQ)),
                  const2((256, 256)), const2((128, 128)), const2((256, 256))],
        out_specs=pl.BlockSpec((TQ, 256), lambda b, i: (b * nq + i, 0)),
        out_shape=jax.ShapeDtypeStruct((batch * seq, 256), BF16),
        scratch_shapes=[pltpu.VMEM((seq, 256), BF16), pltpu.VMEM((seq, 128), BF16),
                        pltpu.VMEM((nc, TK, TQ), F32), pltpu.VMEM((256, IDX_HEADS * TQ), BF16)],
        compiler_params=_attn_params(),
        name="dsa_attn",
    )(bm, bm, cidx, cidx, d, head_mask, k["rk"], k["pvs"], k["one_row"], k["pt4"], k["tril"], k["trilt"],
      k["eye"], k["eyef"], k["big_eye"])


def _mix_kernel(x_ref, g1_ref, wg_ref, o0_ref, o1_ref, o2_ref, o3_ref, wb_ref, wo_ref, y_ref):
    x = x_ref[...]
    hb = _rms(x, g1_ref[...]).astype(BF16)
    mixed = jnp.zeros((TM, D_MODEL), F32)
    for n, o_ref in enumerate((o0_ref, o1_ref, o2_ref, o3_ref)):
        zg = jnp.dot(hb, wg_ref[:, n * D_MODEL:(n + 1) * D_MODEL], preferred_element_type=F32)
        gate = 1.0 / (1.0 + jnp.exp(-zg))
        lifted = jnp.dot(o_ref[...], wb_ref[n], preferred_element_type=F32)
        mixed = mixed + gate * lifted
    y_ref[...] = x + jnp.dot(mixed.astype(BF16), wo_ref[...], preferred_element_type=F32)


def _mix(x2, g1, wg, outs, wb, wo):
    t = x2.shape[0]
    rows = lambda w: pl.BlockSpec((TM, w), lambda i: (i, 0))
    return pl.pallas_call(
        _mix_kernel,
        grid=(t // TM,),
        in_specs=[rows(D_MODEL), pl.BlockSpec((1, D_MODEL), lambda i: (0, 0)),
                  pl.BlockSpec((D_MODEL, 4 * D_MODEL), lambda i: (0, 0)),
                  rows(256), rows(256), rows(256), rows(256),
                  pl.BlockSpec((4, 256, D_MODEL), lambda i: (0, 0, 0)),
                  pl.BlockSpec((D_MODEL, D_MODEL), lambda i: (0, 0))],
        out_specs=rows(D_MODEL),
        out_shape=jax.ShapeDtypeStruct((t, D_MODEL), F32),
        compiler_params=_row_params(),
        name="gate_mix",
    )(x2, g1, wg, *outs, wb, wo)


def _ffn_kernel(final, x_ref, g2_ref, wu_ref, wd_ref, gf_ref, y_ref):
    x = x_ref[...]
    hb = _rms(x, g2_ref[...]).astype(BF16)
    acc = jnp.zeros((TM, D_MODEL), F32)
    for jc in range(D_FF // D_MODEL):
        u = jnp.dot(hb, wu_ref[:, jc * D_MODEL:(jc + 1) * D_MODEL], preferred_element_type=F32)
        u = jnp.square(jnp.maximum(u, 0.0)).astype(BF16)
        acc = acc + jnp.dot(u, wd_ref[jc * D_MODEL:(jc + 1) * D_MODEL, :], preferred_element_type=F32)
    y = x + acc
    if final:
        y = _rms(y, gf_ref[...])
    y_ref[...] = y


def _ffn(x2, g2, wu, wd, gf, final):
    t = x2.shape[0]
    rows = pl.BlockSpec((TM, D_MODEL), lambda i: (i, 0))
    vec = pl.BlockSpec((1, D_MODEL), lambda i: (0, 0))
    return pl.pallas_call(
        functools.partial(_ffn_kernel, final),
        grid=(t // TM,),
        in_specs=[rows, vec, pl.BlockSpec((D_MODEL, D_FF), lambda i: (0, 0)),
                  pl.BlockSpec((D_FF, D_MODEL), lambda i: (0, 0)), vec],
        out_specs=rows,
        out_shape=jax.ShapeDtypeStruct((t, D_MODEL), F32),
        compiler_params=_row_params(),
        name="ffn",
    )(x2, g2, wu, wd, gf)


def _compress_weights(cmp_w, cmp_pe):
    wk = cmp_w[0].reshape(NSA_CMP_LEN, HEAD_DIM, HEAD_DIM)
    wv = cmp_w[1].reshape(NSA_CMP_LEN, HEAD_DIM, HEAD_DIM)
    zero = jnp.zeros_like(wk)
    full = jnp.concatenate([jnp.concatenate([wk, zero], axis=2), jnp.concatenate([zero, wv], axis=2)], axis=1)
    half = NSA_CMP_LEN // 2
    w_top = full[:half].reshape(half * 128, 128).astype(BF16)
    w_bot = full[half:].reshape(half * 128, 128).astype(BF16)
    pe = jnp.concatenate([cmp_pe[0], cmp_pe[1]], axis=1)
    pe2 = pe.reshape(2, half * 128)
    return w_top, w_bot, pe2


def _overlap_matrix(seq):
    n_rows = seq // NSA_CMP_STRIDE
    cs = np.arange(n_rows) * NSA_CMP_STRIDE
    sb = np.arange(LANES) * NSA_SEL_LEN
    ov = np.maximum(np.minimum(cs[:, None] + NSA_CMP_LEN, sb[None, :] + NSA_SEL_LEN)
                    - np.maximum(cs[:, None], sb[None, :]), 0).astype(np.float32) / NSA_CMP_LEN
    ov[:, seq // NSA_SEL_LEN:] = 0.0
    ov[(seq - NSA_CMP_LEN) // NSA_CMP_STRIDE + 1:, :] = 0.0
    return jnp.asarray(np.ascontiguousarray(ov.T))


def kernel(x, positions, norm1_g, w_in, mla_q_norm_g, mla_w_uq, mla_kv_norm_g, mla_w_ukv, nsa_cmp_pe, nsa_cmp_w,
           fox_f_bias, w_branch, w_out, norm2_g, w_up, w_down, final_g):
    batch, seq, _ = x.shape
    depth = w_in.shape[0]
    assert seq % TQ == 0 and seq % TM == 0 and seq // NSA_SEL_LEN <= LANES and TQ == TK
    assert seq // NSA_CMP_STRIDE <= LANES and (seq // NSA_CMP_STRIDE) % 8 == 0

    src, scale = _inproj_layout()
    q_src, k_src, v_src = _mla_up_layouts()
    cs3, place3 = _rope_tables(positions)
    head_mask = _head_masks(N_HEADS, 256)
    consts = _placement_constants(seq)
    eye = jnp.eye(128, dtype=F32)
    overlap_t = _overlap_matrix(seq)

    x2 = x.reshape(batch * seq, D_MODEL)
    for l in range(depth):
        w_packed = _pack_cols(w_in[l, :, :GATE_BASE], src, scale).astype(BF16)
        wg = w_in[l, :, GATE_BASE:].astype(BF16)
        wuq = _pack_cols(mla_w_uq[l], q_src, np.full(512, (MLA_NOPE + MLA_ROPE) ** -0.5 * LOG2E, np.float32)).astype(BF16)
        wuk = _pack_cols(mla_w_ukv[l], k_src, np.ones(512, np.float32)).astype(BF16)
        wuv = _pack_cols(mla_w_ukv[l], v_src, np.ones(256, np.float32)).astype(BF16)
        fb = jnp.zeros((1, 128), F32).at[0, :N_HEADS].set(fox_f_bias[l])
        g1 = norm1_g[l][None, :]

        a, bm, kvc, cidx, d, e = _inproj(x2, g1, w_packed, cs3, place3, mla_q_norm_g[l][None, :], wuq,
                                          mla_kv_norm_g[l][None, :], wuk, wuv, fb, seq)
        w_top, w_bot, pe2 = _compress_weights(nsa_cmp_w[l], nsa_cmp_pe[l])
        kvcmp = _compress(kvc.reshape(batch, seq // NSA_CMP_STRIDE, NSA_CMP_STRIDE * 128), pe2, w_top, w_bot)

        o_mla = _mla(a, consts, batch, seq)
        o_nsa = _nsa(bm, kvcmp, d, head_mask, overlap_t, consts, batch, seq)
        o_fox = _fox(e, d, head_mask, eye, consts, batch, seq)
        o_dsa = _dsa(bm, cidx, d, head_mask, consts, batch, seq)

        x2 = _mix(x2, g1, wg, (o_mla, o_nsa, o_fox, o_dsa), w_branch[l].astype(BF16), w_out[l].astype(BF16))
        x2 = _ffn(x2, norm2_g[l][None, :], w_up[l].astype(BF16), w_down[l].astype(BF16), final_g[None, :],
                  l == depth - 1)
    return x2.reshape(batch, seq, D_MODEL)
```

```python
import functools

import numpy as np
import jax
import jax.numpy as jnp
from jax import lax
from jax.experimental import pallas as pl
from jax.experimental.pallas import tpu as pltpu

F32 = jnp.float32
BF16 = jnp.bfloat16

D_MODEL = 1024
ROPE_THETA = 500000.0
NEG_INF = -1e30
NORM_EPS = 1e-6
HEAD_DIM = 64
N_HEADS = 4
D_FF = 4 * D_MODEL
MLA_NOPE, MLA_ROPE, MLA_V = 64, 32, 64
MLA_Q_LORA, MLA_KV_LORA = 256, 128
NSA_CMP_LEN, NSA_CMP_STRIDE, NSA_SEL_LEN = 32, 16, 64
NSA_TOP_N, NSA_WINDOW, NSA_FORCED_SCORE = 8, 512, 1e4
IDX_HEADS, IDX_DIM = 8, 32
DSA_TOP_K, DSA_KEEP_DIV = 256, 4
MLA_COLS, NSA_COLS, FOX_COLS, DSA_COLS = 416, 652, 772, 680
GATE_BASE = MLA_COLS + NSA_COLS + FOX_COLS + DSA_COLS

LANES = 128
TQ = 256
TK = 256
TM = 256
VMEM_LIMIT = 56 * 1024 * 1024
INT_MIN = -2 ** 31
LOG2E = 1.4426950408889634

P_CQ, P_CKV, P_KR = 0, 256, 384
P_NQ, P_DQ = 512, 768
P_KVC, P_KVS, P_KVW, P_KVD = 1024, 1152, 1280, 1408
P_QI, P_KI = 1536, 1792
P_G, P_W = 2048, 2176
P_FQ, P_FK, P_FV, P_FF = 2304, 2560, 2816, 3072
P_TOTAL = 3200


def _inproj_layout():
    src = -np.ones(P_TOTAL, np.int64)
    scale = np.ones(P_TOTAL, np.float32)
    mla, nsa, fox, dsa = 0, MLA_COLS, MLA_COLS + NSA_COLS, MLA_COLS + NSA_COLS + FOX_COLS
    ar = np.arange
    src[P_CQ:P_CQ + 256] = mla + ar(256)
    src[P_CKV:P_CKV + 128] = mla + 256 + ar(128)
    src[P_KR + 64:P_KR + 96] = mla + 384 + ar(32)
    src[P_NQ:P_NQ + 256] = nsa + ar(256)
    scale[P_NQ:P_NQ + 256] = HEAD_DIM ** -0.5 * LOG2E
    src[P_DQ:P_DQ + 256] = dsa + ar(256)
    scale[P_DQ:P_DQ + 256] = HEAD_DIM ** -0.5 * LOG2E
    for base, k_off, v_off, origin in ((P_KVC, 256, 320, nsa), (P_KVS, 384, 448, nsa),
                                       (P_KVW, 512, 576, nsa), (P_KVD, 256, 320, dsa)):
        src[base:base + 64] = origin + k_off + ar(64)
        src[base + 64:base + 128] = origin + v_off + ar(64)
    src[P_QI:P_QI + 256] = dsa + 384 + ar(256)
    scale[P_QI:P_QI + 256] = (IDX_DIM * IDX_HEADS) ** -0.5
    for r in range(IDX_HEADS):
        src[P_KI + 32 * r:P_KI + 32 * r + 32] = dsa + 640 + ar(32)
    src[P_G:P_G + 12] = nsa + 640 + ar(12)
    src[P_W:P_W + 8] = dsa + 672 + ar(8)
    src[P_FQ:P_FQ + 256] = fox + ar(256)
    scale[P_FQ:P_FQ + 256] = HEAD_DIM ** -0.5 * LOG2E
    src[P_FK:P_FK + 256] = fox + 256 + ar(256)
    src[P_FV:P_FV + 256] = fox + 512 + ar(256)
    src[P_FF:P_FF + 4] = fox + 768 + ar(4)
    return src, scale


def _pack_cols(w, src, scale):
    pieces, start = [], 0
    for end in range(1, len(src) + 1):
        run_ends = (end == len(src) or scale[end] != scale[start]
                    or (src[end] != src[end - 1] + 1 if src[start] >= 0 else src[end] >= 0))
        if run_ends:
            if src[start] >= 0:
                pieces.append(w[:, int(src[start]):int(src[start]) + end - start] * float(scale[start]))
            else:
                pieces.append(jnp.zeros((w.shape[0], end - start), w.dtype))
            start = end
    return jnp.concatenate(pieces, axis=1)


def _mla_up_layouts():
    q_src = -np.ones(512, np.int64)
    k_src = -np.ones(512, np.int64)
    v_src = np.zeros(256, np.int64)
    for h in range(N_HEADS):
        q_src[h * 128:h * 128 + 96] = h * 96 + np.arange(96)
        k_src[h * 128:h * 128 + 64] = h * 128 + np.arange(64)
        v_src[h * 64:h * 64 + 64] = h * 128 + 64 + np.arange(64)
    return q_src, k_src, v_src


def _rope_tables(positions):
    pos = positions.astype(F32).reshape(-1)[None, :]
    n = pos.shape[1]
    rots = (("mla", MLA_ROPE), ("head", HEAD_DIM // 4), ("idx", IDX_DIM // 4))
    inv = jnp.concatenate([ROPE_THETA ** (-jnp.arange(0, rot, 2, dtype=F32) / rot) for _, rot in rots])
    ang = inv[:, None] * pos
    cos_all, sin_all = jnp.cos(ang), jnp.sin(ang)
    parts, offs = [], {}
    col = row = 0
    for name, rot in rots:
        parts += [cos_all[row:row + rot // 2], sin_all[row:row + rot // 2]]
        offs[name] = (col, col + rot // 2, rot // 2)
        col += rot
        row += rot // 2
    one_col = col
    parts.append(jnp.ones((64 - col, n), F32))
    cs = jnp.concatenate(parts, axis=0)
    hi = cs.astype(BF16)
    rest = cs - hi.astype(F32)
    mid = rest.astype(BF16)
    low = (rest - mid.astype(F32)).astype(BF16)
    cs3 = jnp.concatenate([hi, mid, low, jnp.zeros((64, n), BF16)], axis=0).T

    place = np.zeros((64, 768), np.float32)

    def fill(cos_base, sin_base, name, group, first, n_groups):
        c0, s0, half = offs[name]
        place[one_col, cos_base:cos_base + 128] = 1.0
        for g in range(n_groups):
            lo = g * group + first
            for j in range(half):
                for lane, sign in ((lo + j, -1.0), (lo + half + j, 1.0)):
                    place[one_col, cos_base + lane] = 0.0
                    place[c0 + j, cos_base + lane] = 1.0
                    place[s0 + j, sin_base + lane] = sign

    fill(0, 128, "mla", 128, 64, 1)
    fill(256, 384, "head", 64, 0, 2)
    fill(512, 640, "idx", 32, 0, 4)
    place3 = np.concatenate([place, place, place, np.zeros((64, 768), np.float32)], axis=0)
    return cs3, jnp.asarray(place3, BF16)


def _head_masks(n_heads, width):
    lane = np.arange(width)
    return jnp.asarray((lane[None, :] // (width // n_heads) == np.arange(n_heads)[:, None]), BF16)


def _placement_constants(seq):
    d = np.arange(64)
    rk = np.zeros((128, 256), np.float32)
    pvs = np.zeros((128, 128), np.float32)
    pv4 = np.zeros((4, 256, 128), np.float32)
    pt4 = np.zeros((4, 128, 256), np.float32)
    pvs[64 + d, d] = 1.0
    for h in range(N_HEADS):
        rk[d, h * 64 + d] = 1.0
        pv4[h, h * 64 + d, d] = 1.0
        pt4[h, d, h * 64 + d] = 1.0
    one_row = np.zeros((1, 128), np.float32)
    one_row[0, 64] = 1.0
    r, c = np.arange(TQ)[:, None], np.arange(TK)[None, :]
    tril = np.where(c <= r, 0.0, NEG_INF).astype(np.float32)
    triu = np.where(c > r, 0.0, NEG_INF).astype(np.float32)
    void, free = np.full_like(tril, NEG_INF), np.zeros_like(tril)
    win_bias = np.stack([np.concatenate(t, axis=1) for t in ((tril, void, void), (free, tril, void),
                                                              (triu, free, tril))])
    nc = seq // TK
    key_blk = (np.arange(seq) // NSA_SEL_LEN).reshape(nc, 1, TK)
    expand = (key_blk == np.arange(LANES).reshape(1, LANES, 1)).astype(np.float32)
    bf = lambda a: jnp.asarray(a, BF16)
    return dict(rk=bf(rk), pvs=bf(pvs), pv4=bf(pv4), pt4=bf(pt4), one_row=jnp.asarray(one_row),
                tril=jnp.asarray(tril), win_bias=jnp.asarray(win_bias), expand=bf(expand * -NEG_INF),
                eye=bf(np.eye(256, dtype=np.float32)), big_eye=bf(np.eye(256, dtype=np.float32) * -NEG_INF),
                eyef=jnp.asarray(np.eye(128, dtype=np.float32)),
                trilt=jnp.asarray(np.ascontiguousarray(tril.T)))


def _rms(x, g):
    return x * lax.rsqrt(jnp.mean(x * x, axis=-1, keepdims=True) + NORM_EPS) * g


def _rope_tile(x, cos, sin, is_x1, half):
    fwd = pltpu.roll(x, LANES - half, axis=1)
    bwd = pltpu.roll(x, half, axis=1)
    return x * cos + jnp.where(is_x1, fwd, bwd) * sin


def _nt_dot(a, b):
    return lax.dot_general(a, b, (((1,), (1,)), ((), ())), preferred_element_type=F32)


def _attn_params():
    return pltpu.CompilerParams(dimension_semantics=("arbitrary", "arbitrary"), vmem_limit_bytes=VMEM_LIMIT)


def _row_params():
    return pltpu.CompilerParams(dimension_semantics=("arbitrary",), vmem_limit_bytes=VMEM_LIMIT)


def _inproj_kernel(tiles_per_seq, x_ref, g1_ref, w_ref, cs_ref, place_ref, gq_ref, wuq_ref, gkv_ref, wuk_ref,
                   wuv_ref, fb_ref, a_ref, b_ref, kvc_ref, c_ref, d_ref, e_ref, carry_ref):
    i = pl.program_id(0)
    x = x_ref[...]
    hb = _rms(x, g1_ref[...]).astype(BF16)
    z = jnp.dot(hb, w_ref[...], preferred_element_type=F32)

    lane = lax.broadcasted_iota(jnp.int32, (TM, LANES), 1)
    tab = jnp.dot(cs_ref[...], place_ref[...], preferred_element_type=F32)
    cos1, sin1 = tab[:, 0:128], tab[:, 128:256]
    cos3, sin3 = tab[:, 256:384], tab[:, 384:512]
    cos5, sin5 = tab[:, 512:640], tab[:, 640:768]
    x1_mla = (lane >= 64) & (lane < 80)
    x1_head = (lane & 63) < 8
    x1_idx = (lane & 31) < 4
    k_half = lane < 64
    cos4 = jnp.where(k_half, cos3, 1.0)
    sin4 = jnp.where(k_half, sin3, 0.0)

    cqn = _rms(z[:, P_CQ:P_CQ + 256], gq_ref[...]).astype(BF16)
    q_up = jnp.dot(cqn, wuq_ref[...], preferred_element_type=F32)
    ckvn = _rms(z[:, P_CKV:P_CKV + 128], gkv_ref[...]).astype(BF16)
    k_up = jnp.dot(ckvn, wuk_ref[...], preferred_element_type=F32)
    v_up = jnp.dot(ckvn, wuv_ref[...], preferred_element_type=F32)
    k_rot = _rope_tile(z[:, P_KR:P_KR + 128], cos1, sin1, x1_mla, 16)
    for h in range(N_HEADS):
        sl = slice(h * 128, (h + 1) * 128)
        a_ref[:, sl] = _rope_tile(q_up[:, sl], cos1, sin1, x1_mla, 16).astype(BF16)
        a_ref[:, 512 + h * 128:512 + (h + 1) * 128] = (k_up[:, sl] + k_rot).astype(BF16)
    a_ref[:, 1024:1280] = v_up.astype(BF16)

    for t in range(4):
        b_ref[:, t * 128:(t + 1) * 128] = _rope_tile(
            z[:, P_NQ + t * 128:P_NQ + (t + 1) * 128], cos3, sin3, x1_head, 8).astype(BF16)
    kvc_ref[...] = _rope_tile(z[:, P_KVC:P_KVC + 128], cos4, sin4, x1_head & k_half, 8).astype(BF16)
    for t, off in enumerate((P_KVS, P_KVW, P_KVD)):
        b_ref[:, 512 + t * 128:512 + (t + 1) * 128] = _rope_tile(
            z[:, off:off + 128], cos4, sin4, x1_head & k_half, 8).astype(BF16)

    for t in range(4):
        c_ref[:, t * 128:(t + 1) * 128] = _rope_tile(
            z[:, P_QI + t * 128:P_QI + (t + 1) * 128], cos5, sin5, x1_idx, 4).astype(BF16)

    d_ref[:, 0:128] = 1.0 / (1.0 + jnp.exp(-z[:, P_G:P_G + 128]))
    d_ref[:, 128:256] = z[:, P_W:P_W + 128]
    f = z[:, P_FF:P_FF + 128] + fb_ref[...]
    log_f = jnp.minimum(f, 0.0) - jnp.log(1.0 + jnp.exp(-jnp.abs(f)))

    @pl.when(i % tiles_per_seq == 0)
    def _():
        carry_ref[...] = jnp.zeros_like(carry_ref)

    row = lax.broadcasted_iota(jnp.int32, (TM, TM), 0)
    col = lax.broadcasted_iota(jnp.int32, (TM, TM), 1)
    tri = jnp.where(col <= row, 1.0, 0.0).astype(F32)
    c = jnp.dot(tri, log_f, preferred_element_type=F32, precision=lax.Precision.HIGHEST)
    c = c + carry_ref[0:1, :]
    d_ref[:, 256:384] = c
    carry_ref[...] = jnp.broadcast_to(c[TM - 1:TM, :], carry_ref.shape)

    e_ref[...] = z[:, P_FQ:P_FQ + 768].astype(BF16)


def _inproj(x2, g1, w_packed, cs3, place3, gq, wuq, gkv, wuk, wuv, fb, seq):
    t = x2.shape[0]
    full = lambda shape: pl.BlockSpec(shape, lambda i: (0, 0))
    rows = lambda w: pl.BlockSpec((TM, w), lambda i: (i, 0))
    return pl.pallas_call(
        functools.partial(_inproj_kernel, seq // TM),
        grid=(t // TM,),
        in_specs=[rows(D_MODEL), full((1, D_MODEL)), full((D_MODEL, P_TOTAL)), rows(256), full((256, 768)),
                  full((1, 256)), full((256, 512)), full((1, 128)), full((128, 512)), full((128, 256)),
                  full((1, 128))],
        out_specs=[rows(1280), rows(896), rows(128), rows(512), rows(384), rows(768)],
        out_shape=[jax.ShapeDtypeStruct((t, 1280), BF16), jax.ShapeDtypeStruct((t, 896), BF16),
                   jax.ShapeDtypeStruct((t, 128), BF16), jax.ShapeDtypeStruct((t, 512), BF16),
                   jax.ShapeDtypeStruct((t, 384), F32), jax.ShapeDtypeStruct((t, 768), BF16)],
        scratch_shapes=[pltpu.VMEM((8, 128), F32)],
        compiler_params=_row_params(),
        name="inproj",
    )(x2, g1, w_packed, cs3, place3, gq, wuq, gkv, wuk, wuv, fb)


def _compress_kernel(r_ref, pe_ref, wt_ref, wb_ref, o_ref):
    r = r_ref[0].astype(F32)
    top = jnp.dot((r + pe_ref[0:1, :]).astype(BF16), wt_ref[...], preferred_element_type=F32)
    bot = jnp.dot((r + pe_ref[1:2, :]).astype(BF16), wb_ref[...], preferred_element_type=F32)
    n = bot.shape[0]
    o_ref[0] = (top + pltpu.roll(bot, n - 1, axis=0)).astype(BF16)


def _compress(kvc3, pe2, w_top, w_bot):
    b, n, _ = kvc3.shape
    return pl.pallas_call(
        _compress_kernel,
        grid=(b,),
        in_specs=[pl.BlockSpec((1, n, 2048), lambda i: (i, 0, 0)), pl.BlockSpec((2, 2048), lambda i: (0, 0)),
                  pl.BlockSpec((2048, 128), lambda i: (0, 0)), pl.BlockSpec((2048, 128), lambda i: (0, 0))],
        out_specs=pl.BlockSpec((1, n, 128), lambda i: (i, 0, 0)),
        out_shape=jax.ShapeDtypeStruct((b, n, 128), BF16),
        compiler_params=_row_params(),
        name="nsa_compress",
    )(kvc3, pe2, w_top, w_bot)


def _flash_init(rows):
    return jnp.full((rows, 1), NEG_INF, F32), jnp.zeros((rows, LANES), F32)


def _flash_update(carry, s, v):
    m, acc = carry
    m_new = jnp.maximum(m, jnp.max(s, axis=-1, keepdims=True))
    p = jnp.exp2(s - m_new).astype(BF16)
    alpha = jnp.exp2(m - m_new)
    return m_new, alpha * acc + jnp.dot(p, v, preferred_element_type=F32)


def _flash_update_many(carries, scores, values):
    ms = [jnp.maximum(m, jnp.max(s, axis=-1, keepdims=True)) for (m, _), s in zip(carries, scores)]
    ps = [jnp.exp2(s - m).astype(BF16) for s, m in zip(scores, ms)]
    alphas = [jnp.exp2(m_old - m) for (m_old, _), m in zip(carries, ms)]
    pvs = [jnp.dot(p, v, preferred_element_type=F32) for p, v in zip(ps, values)]
    return tuple((m, a * acc + pv) for m, a, (_, acc), pv in zip(ms, alphas, carries, pvs))


def _key_group_loop(n, step, init):
    n4, r = n // 4, n % 4
    as_trips = lambda pred: pred.astype(jnp.int32)
    carry = lax.fori_loop(0, n4 - as_trips(r == 0), lambda g, cr: step(4 * g, 4, cr, False), init)
    carry = lax.fori_loop(0, as_trips(r == 0), lambda _, cr: step(n - 4, 4, cr, True), carry)
    carry = lax.fori_loop(0, as_trips(r == 3), lambda _, cr: step(n - 3, 2, cr, False), carry)
    carry = lax.fori_loop(0, as_trips(r == 2), lambda _, cr: step(n - 2, 2, cr, True), carry)
    return lax.fori_loop(0, as_trips((r == 1) | (r == 3)), lambda _, cr: step(n - 1, 1, cr, True), carry)


def _add_diag(s, tril):
    if s.shape[1] == TK:
        return s + tril
    return jnp.concatenate([s[:, :s.shape[1] - TK], s[:, s.shape[1] - TK:] + tril], axis=1)


def _flash_out(acc):
    return acc / jnp.maximum(acc[:, 64:65], 1e-30)


def _place_heads(outs, pt_ref):
    res = None
    for h, o in enumerate(outs):
        t = jnp.dot(o.astype(BF16), pt_ref[h], preferred_element_type=F32)
        res = t if res is None else res + t
    return res.astype(BF16)


def _chunk_rows(c, width=1):
    return pl.ds(pl.multiple_of(c * TK, TK), width * TK)


def _stack_heads(x, mask_ref, n):
    return jnp.concatenate([x * mask_ref[h:h + 1, :] for h in range(n)], axis=0)


def _mla_kernel(q_ref, k_ref, v_ref, pv_ref, one_ref, pt_ref, tril_ref, o_ref, vaug_ref):
    i = pl.program_id(1)

    @pl.when(i == 0)
    def _():
        v = v_ref[...]
        for h in range(N_HEADS):
            vaug_ref[h] = (jnp.dot(v, pv_ref[h], preferred_element_type=F32) + one_ref[...]).astype(BF16)

    qs = [q_ref[:, h * 128:(h + 1) * 128] for h in range(N_HEADS)]

    def step(c, width, carries, diag):
        rows = _chunk_rows(c, width)
        scores = [_nt_dot(qs[h], k_ref[rows, h * 128:(h + 1) * 128]) for h in range(N_HEADS)]
        if diag:
            scores = [_add_diag(s, tril_ref[...]) for s in scores]
        return _flash_update_many(carries, scores, [vaug_ref[h, rows, :] for h in range(N_HEADS)])

    carries = _key_group_loop(i + 1, step, tuple(_flash_init(TQ) for _ in range(N_HEADS)))
    o_ref[...] = _place_heads([_flash_out(acc) for _, acc in carries], pt_ref)


def _mla(a, k, batch, seq):
    nq = seq // TQ
    const2 = lambda shape: pl.BlockSpec(shape, lambda b, i: (0, 0))
    const3 = lambda shape: pl.BlockSpec(shape, lambda b, i: (0, 0, 0))
    return pl.pallas_call(
        _mla_kernel,
        grid=(batch, nq),
        in_specs=[pl.BlockSpec((TQ, 512), lambda b, i: (b * nq + i, 0)),
                  pl.BlockSpec((seq, 512), lambda b, i: (b, 1)),
                  pl.BlockSpec((seq, 256), lambda b, i: (b, 4)),
                  const3((4, 256, 128)), const2((1, 128)), const3((4, 128, 256)), const2((TQ, TK))],
        out_specs=pl.BlockSpec((TQ, 256), lambda b, i: (b * nq + i, 0)),
        out_shape=jax.ShapeDtypeStruct((batch * seq, 256), BF16),
        scratch_shapes=[pltpu.VMEM((N_HEADS, seq, 128), BF16)],
        compiler_params=_attn_params(),
        name="mla_attn",
    )(a, a, a, k["pv4"], k["one_row"], k["pt4"], k["tril"])


def _fox_kernel(n_chunks, q_ref, k_ref, v_ref, call_ref, hm_ref, eye_ref, pv_ref, one_ref, pt_ref,
                tril_ref, o_ref, vaug_ref, ct_ref):
    i = pl.program_id(1)

    @pl.when(i == 0)
    def _():
        v = v_ref[...]
        for h in range(N_HEADS):
            vaug_ref[h] = (jnp.dot(v, pv_ref[h], preferred_element_type=F32) + one_ref[...]).astype(BF16)
        ct = lax.dot_general(eye_ref[...], call_ref[...], (((1,), (1,)), ((), ())),
                             preferred_element_type=F32, precision=lax.Precision.HIGHEST)
        for c in range(n_chunks):
            ct_ref[c] = ct[0:8, c * TK:(c + 1) * TK] * LOG2E

    q_all = q_ref[...]
    qs = [q_all * hm_ref[h:h + 1, :] for h in range(N_HEADS)]

    def step(c, width, carries, diag):
        rows = _chunk_rows(c, width)
        k = k_ref[rows, :]
        cs = jnp.concatenate([ct_ref[c + u] for u in range(width)], axis=1)
        scores = [_nt_dot(qs[h], k) - cs[h:h + 1, :] for h in range(N_HEADS)]
        if diag:
            scores = [_add_diag(s, tril_ref[...]) for s in scores]
        return _flash_update_many(carries, scores, [vaug_ref[h, rows, :] for h in range(N_HEADS)])

    carries = _key_group_loop(i + 1, step, tuple(_flash_init(TQ) for _ in range(N_HEADS)))
    o_ref[...] = _place_heads([_flash_out(acc) for _, acc in carries], pt_ref)


def _fox(e, d, head_mask, eye, k, batch, seq):
    nq = seq // TQ
    nc = seq // TK
    const2 = lambda shape: pl.BlockSpec(shape, lambda b, i: (0, 0))
    const3 = lambda shape: pl.BlockSpec(shape, lambda b, i: (0, 0, 0))
    return pl.pallas_call(
        functools.partial(_fox_kernel, nc),
        grid=(batch, nq),
        in_specs=[pl.BlockSpec((TQ, 256), lambda b, i: (b * nq + i, 0)),
                  pl.BlockSpec((seq, 256), lambda b, i: (b, 1)),
                  pl.BlockSpec((seq, 256), lambda b, i: (b, 2)),
                  pl.BlockSpec((seq, 128), lambda b, i: (b, 2)),
                  const2((N_HEADS, 256)), const2((128, 128)),
                  const3((4, 256, 128)), const2((1, 128)), const3((4, 128, 256)), const2((TQ, TK))],
        out_specs=pl.BlockSpec((TQ, 256), lambda b, i: (b * nq + i, 0)),
        out_shape=jax.ShapeDtypeStruct((batch * seq, 256), BF16),
        scratch_shapes=[pltpu.VMEM((N_HEADS, seq, 128), BF16), pltpu.VMEM((nc, 8, TK), F32)],
        compiler_params=_attn_params(),
        name="fox_attn",
    )(e, e, e, d, head_mask, eye, k["pv4"], k["one_row"], k["pt4"], k["tril"])


def _nsa_kernel(n_sel_blocks, q_ref, kvs_ref, kvw_ref, kvc_ref, g_ref, hm_ref, rk_ref, pvs_ref, one_ref,
                pt_ref, tril_ref, winb_ref, ex_ref, ovt_ref, eye_ref, o_ref, ks_ref, vs_ref, kw_ref, vw_ref, kc_ref,
                vc_ref):
    i = pl.program_id(1)

    @pl.when(i == 0)
    def _():
        rk, pvs, one = rk_ref[...], pvs_ref[...], one_ref[...]
        for src_ref, k_dst, v_dst in ((kvs_ref, ks_ref, vs_ref), (kvw_ref, kw_ref, vw_ref)):
            kv = src_ref[...]
            k_dst[...] = jnp.dot(kv, rk, preferred_element_type=F32).astype(BF16)
            v_dst[...] = (jnp.dot(kv, pvs, preferred_element_type=F32) + one).astype(BF16)
        kv = kvc_ref[0]
        kc_ref[...] = jnp.dot(kv, rk, preferred_element_type=F32).astype(BF16)
        vc_ref[...] = jnp.dot(kv, pvs, preferred_element_type=F32).astype(BF16)

    g = g_ref[...]
    q_stack = _stack_heads(q_ref[...], hm_ref, N_HEADS)
    n_cmp_rows = kc_ref.shape[0]
    head = lambda x, h: x[h * TQ:(h + 1) * TQ]

    t_c = i * TQ + lax.broadcasted_iota(jnp.int32, (TQ, n_cmp_rows), 0)
    c_idx = lax.broadcasted_iota(jnp.int32, (TQ, n_cmp_rows), 1)
    cmp_mask = c_idx * NSA_CMP_STRIDE + (NSA_CMP_LEN - 1) <= t_c
    s_cmp = _nt_dot(q_stack, kc_ref[...])
    p_sum = jnp.zeros((TQ, n_cmp_rows), F32)
    o_cmp = []
    for h in range(N_HEADS):
        s = jnp.where(cmp_mask, head(s_cmp, h), NEG_INF)
        m = jnp.max(s, axis=-1, keepdims=True)
        p = jnp.where(cmp_mask, jnp.exp2(s - m), 0.0)
        p = p / jnp.maximum(jnp.sum(p, axis=-1, keepdims=True), 1e-30)
        p_sum = p_sum + p
        o_cmp.append(jnp.dot(p.astype(BF16), vc_ref[...], preferred_element_type=F32))

    imp = lax.dot_general(ovt_ref[...], p_sum, (((1,), (1,)), ((), ())), preferred_element_type=F32,
                          precision=lax.Precision.HIGHEST)[0:n_sel_blocks]
    t_r = i * TQ + lax.broadcasted_iota(jnp.int32, (n_sel_blocks, TQ), 1)
    j = lax.broadcasted_iota(jnp.int32, (n_sel_blocks, TQ), 0)
    t_blk = t_r >> 6
    forced = (j == 0) | (j == t_blk) | (j == t_blk - 1)
    imp = jnp.where(forced, NSA_FORCED_SCORE, imp)
    imp = jnp.where(j * NSA_SEL_LEN <= t_r, imp, NEG_INF)
    rank = jnp.zeros((n_sel_blocks, TQ), F32)
    for i2 in range(n_sel_blocks):
        row = imp[i2:i2 + 1, :]
        beats = jnp.where(row > imp, 1.0, jnp.where(row == imp, jnp.where(j > i2, 1.0, 0.0), 0.0))
        rank = rank + beats
    dropped_t = jnp.concatenate([jnp.where(rank < float(NSA_TOP_N), 0.0, -1.0),
                                 jnp.full((LANES - n_sel_blocks, TQ), -1.0, F32)], axis=0).astype(BF16)
    dropped = _nt_dot(eye_ref[...], dropped_t).astype(BF16)

    def sel_step(c, width, carry, diag):
        rows = _chunk_rows(c, width)
        expand = jnp.concatenate([ex_ref[c + u] for u in range(width)], axis=1)
        pen = jnp.dot(dropped, expand, preferred_element_type=F32)
        if diag:
            pen = _add_diag(pen, tril_ref[...])
        s_all = _nt_dot(q_stack, ks_ref[rows, :])
        s = jnp.concatenate([head(s_all, h) + pen for h in range(N_HEADS)], axis=0)
        return _flash_update(carry, s, vs_ref[rows, :])

    _, sel_acc = _key_group_loop(i + 1, sel_step, _flash_init(N_HEADS * TQ))

    rows = _chunk_rows(jnp.maximum(i - 2, 0), 3)
    pen = winb_ref[jnp.minimum(i, 2)]
    s_all = _nt_dot(q_stack, kw_ref[rows, :])
    s = jnp.concatenate([head(s_all, h) + pen for h in range(N_HEADS)], axis=0)
    _, win_acc = _flash_update(_flash_init(N_HEADS * TQ), s, vw_ref[rows, :])

    o_sel = _flash_out(sel_acc)
    o_win = _flash_out(win_acc)
    outs = []
    for h in range(N_HEADS):
        outs.append(g[:, h:h + 1] * o_cmp[h] + g[:, 4 + h:5 + h] * head(o_sel, h)
                    + g[:, 8 + h:9 + h] * head(o_win, h))
    o_ref[...] = _place_heads(outs, pt_ref)


def _nsa(bm, kvcmp, d, head_mask, overlap_t, k, batch, seq):
    nq = seq // TQ
    nc = seq // TK
    n_cmp_rows = kvcmp.shape[1]
    const2 = lambda shape: pl.BlockSpec(shape, lambda b, i: (0, 0))
    const3 = lambda shape: pl.BlockSpec(shape, lambda b, i: (0, 0, 0))
    return pl.pallas_call(
        functools.partial(_nsa_kernel, seq // NSA_SEL_LEN),
        grid=(batch, nq),
        in_specs=[pl.BlockSpec((TQ, 256), lambda b, i: (b * nq + i, 0)),
                  pl.BlockSpec((seq, 128), lambda b, i: (b, 4)),
                  pl.BlockSpec((seq, 128), lambda b, i: (b, 5)),
                  pl.BlockSpec((1, n_cmp_rows, 128), lambda b, i: (b, 0, 0)),
                  pl.BlockSpec((TQ, 128), lambda b, i: (b * nq + i, 0)),
                  const2((N_HEADS, 256)), const2((128, 256)), const2((128, 128)), const2((1, 128)),
                  const3((4, 128, 256)), const2((TQ, TK)), const3((3, TQ, 3 * TK)), const3((nc, LANES, TK)),
                  const2((LANES, n_cmp_rows)), const2((256, 256))],
        out_specs=pl.BlockSpec((TQ, 256), lambda b, i: (b * nq + i, 0)),
        out_shape=jax.ShapeDtypeStruct((batch * seq, 256), BF16),
        scratch_shapes=[pltpu.VMEM((seq, 256), BF16), pltpu.VMEM((seq, 128), BF16),
                        pltpu.VMEM((seq, 256), BF16), pltpu.VMEM((seq, 128), BF16),
                        pltpu.VMEM((n_cmp_rows, 256), BF16), pltpu.VMEM((n_cmp_rows, 128), BF16)],
        compiler_params=_attn_params(),
        name="nsa_attn",
    )(bm, bm, bm, kvcmp, d, head_mask, k["rk"], k["pvs"], k["one_row"], k["pt4"], k["tril"], k["win_bias"],
      k["expand"], overlap_t, k["eye"])


def _dsa_kernel(n_keep, idx_bits, q_ref, kvd_ref, qi_ref, ki_ref, w_ref, hm_ref, rk_ref, pvs_ref,
                one_ref, pt_ref, tril_ref, trilt_ref, eye_ref, eyef_ref, bigeye_ref, o_ref, kd_ref, vd_ref, sc_ref,
                qit_ref):
    i = pl.program_id(1)

    @pl.when(i == 0)
    def _():
        kv = kvd_ref[...]
        kd_ref[...] = jnp.dot(kv, rk_ref[...], preferred_element_type=F32).astype(BF16)
        vd_ref[...] = (jnp.dot(kv, pvs_ref[...], preferred_element_type=F32) + one_ref[...]).astype(BF16)

    head = lambda x, h: x[h * TQ:(h + 1) * TQ]

    qi_t = _nt_dot(eye_ref[...], qi_ref[...])
    feat = lax.broadcasted_iota(jnp.int32, qi_t.shape, 0)
    for h in range(IDX_HEADS):
        qit_ref[:, h * TQ:(h + 1) * TQ] = jnp.where((feat >> 5) == h, qi_t, 0.0).astype(BF16)
    w_t = lax.dot_general(eyef_ref[...], w_ref[...], (((1,), (1,)), ((), ())),
                          preferred_element_type=F32, precision=lax.Precision.HIGHEST)
    ws = [w_t[h:h + 1, :] for h in range(IDX_HEADS)]

    def score_chunk(c, bias):
        logits = jnp.dot(ki_ref[_chunk_rows(c), :], qit_ref[...], preferred_element_type=F32)
        score = jnp.maximum(logits[:, 0:TQ], 0.0) * ws[0]
        for h in range(1, IDX_HEADS):
            score = score + jnp.maximum(logits[:, h * TQ:(h + 1) * TQ], 0.0) * ws[h]
        if bias is not None:
            score = score + bias
        sc_ref[c] = score

    def score_body(c, carry):
        score_chunk(c, None)
        return carry

    lax.fori_loop(0, i, score_body, 0)
    score_chunk(i, trilt_ref[...])

    def count(pred):
        def body(c, part):
            hit = jnp.where(pred(sc_ref[c], c), 1.0, 0.0)
            return part + jnp.sum(hit.reshape(TK // 8, 8, TQ), axis=0)
        part = lax.fori_loop(0, i + 1, body, jnp.zeros((8, TQ), F32))
        return jnp.sum(part, axis=0, keepdims=True)

    def key_to_float(key):
        bits = jnp.where(key < 0, key ^ jnp.int32(0x7FFFFFFF), key)
        return lax.bitcast_convert_type(bits, F32)

    keep = float(n_keep)
    total = ((i + 1) * TK).astype(F32)
    cnt0 = count(lambda s, c: s >= 0.0)
    thr0 = jnp.where(cnt0 >= keep, jnp.int32(0), jnp.int32(INT_MIN))
    cnt0 = jnp.where(cnt0 >= keep, cnt0, total)

    def thr_bit(it, carry):
        thr, cnt_thr = carry
        cand = thr | jnp.left_shift(jnp.int32(1), 30 - it)
        cand_f = key_to_float(cand)
        cnt = count(lambda s, c: s >= cand_f)
        take = cnt >= keep
        return jnp.where(take, cand, thr), jnp.where(take, cnt, cnt_thr)

    thr, cnt_thr = lax.fori_loop(0, 31, thr_bit, (thr0, cnt0))
    thr_f = key_to_float(thr)

    def key_index(c):
        return c * TK + lax.broadcasted_iota(jnp.int32, (TK, TQ), 0)

    def resolve_ties(_):
        need = keep - count(lambda s, c: s > thr_f)

        def cut_bit(it, cut):
            cand = cut | jnp.left_shift(jnp.int32(1), idx_bits - 1 - it)
            cnt = count(lambda s, c: (s == thr_f) & (key_index(c) < cand))
            return jnp.where(cnt < need, cand, cut)

        return lax.fori_loop(0, idx_bits, cut_bit, jnp.zeros((1, TQ), jnp.int32))

    has_excess = jnp.max(jnp.abs(cnt_thr - keep)) > 0.0
    cut = lax.cond(has_excess, resolve_ties, lambda _: jnp.full((1, TQ), 2 ** idx_bits, jnp.int32), 0)

    q_stack = _stack_heads(q_ref[...], hm_ref, N_HEADS)
    big_eye = bigeye_ref[...]

    def attn_step(c, width, carry, diag):
        rows = _chunk_rows(c, width)
        sc = jnp.concatenate([sc_ref[c + u] for u in range(width)], axis=0)
        k_idx = c * TK + lax.broadcasted_iota(jnp.int32, sc.shape, 0)
        tie = jnp.where(k_idx <= cut, 0.0, -1.0)
        dropped_t = jnp.where(sc > thr_f, 0.0, jnp.where(sc == thr_f, tie, -1.0)).astype(BF16)
        pen = _nt_dot(big_eye, dropped_t)
        if diag:
            pen = _add_diag(pen, tril_ref[...])
        s_all = _nt_dot(q_stack, kd_ref[rows, :])
        s = jnp.concatenate([head(s_all, h) + pen for h in range(N_HEADS)], axis=0)
        return _flash_update(carry, s, vd_ref[rows, :])

    _, acc = _key_group_loop(i + 1, attn_step, _flash_init(N_HEADS * TQ))
    out = _flash_out(acc)
    o_ref[...] = _place_heads([head(out, h) for h in range(N_HEADS)], pt_ref)


def _dsa(bm, cidx, d, head_mask, k, batch, seq):
    nq = seq // TQ
    nc = seq // TK
    n_keep = min(DSA_TOP_K, seq // DSA_KEEP_DIV)
    assert n_keep <= TK
    idx_bits = int(np.log2(seq))
    assert 2 ** idx_bits == seq
    const2 = lambda shape: pl.BlockSpec(shape, lambda b, i: (0, 0))
    const3 = lambda shape: pl.BlockSpec(shape, lambda b, i: (0, 0, 0))
    return pl.pallas_call(
        functools.partial(_dsa_kernel, n_keep, idx_bits),
        grid=(batch, nq),
        in_specs=[pl.BlockSpec((TQ, 256), lambda b, i: (b * nq + i, 1)),
                  pl.BlockSpec((seq, 128), lambda b, i: (b, 6)),
                  pl.BlockSpec((TQ, 256), lambda b, i: (b * nq + i, 0)),
                  pl.BlockSpec((seq, 256), lambda b, i: (b, 1)),
                  pl.BlockSpec((TQ, 128), lambda b, i: (b * nq + i, 1)),
                  const2((N_HEADS, 256)), const2((128, 256)), const2((128, 128)),
                  const2((1, 128)), const3((4, 128, 256)), const2((TQ, TK)), const2((TK, TQ)),
                  const2((256, 256)), const2((128, 128)), const2((256, 256))],
        out_specs=pl.BlockSpec((TQ, 256), lambda b, i: (b * nq + i, 0)),
        out_shape=jax.ShapeDtypeStruct((batch * seq, 256), BF16),
        scratch_shapes=[pltpu.VMEM((seq, 256), BF16), pltpu.VMEM((seq, 128), BF16),
                        pltpu.VMEM((nc, TK, TQ), F32), pltpu.VMEM((256, IDX_HEADS * TQ), BF16)],
        compiler_params=_attn_params(),
        name="dsa_attn",
    )(bm, bm, cidx, cidx, d, head_mask, k["rk"], k["pvs"], k["one_row"], k["pt4"], k["tril"], k["trilt"],
      k["eye"], k["eyef"], k["big_eye"])


def _mix_kernel(x_ref, g1_ref, wg_ref, o0_ref, o1_ref, o2_ref, o3_ref, wb_ref, wo_ref, y_ref):
    x = x_ref[...]
    hb = _rms(x, g1_ref[...]).astype(BF16)
    mixed = jnp.zeros((TM, D_MODEL), F32)
    for n, o_ref in enumerate((o0_ref, o1_ref, o2_ref, o3_ref)):
        zg = jnp.dot(hb, wg_ref[:, n * D_MODEL:(n + 1) * D_MODEL], preferred_element_type=F32)
        gate = 1.0 / (1.0 + jnp.exp(-zg))
        lifted = jnp.dot(o_ref[...], wb_ref[n], preferred_element_type=F32)
        mixed = mixed + gate * lifted
    y_ref[...] = x + jnp.dot(mixed.astype(BF16), wo_ref[...], preferred_element_type=F32)


def _mix(x2, g1, wg, outs, wb, wo):
    t = x2.shape[0]
    rows = lambda w: pl.BlockSpec((TM, w), lambda i: (i, 0))
    return pl.pallas_call(
        _mix_kernel,
        grid=(t // TM,),
        in_specs=[rows(D_MODEL), pl.BlockSpec((1, D_MODEL), lambda i: (0, 0)),
                  pl.BlockSpec((D_MODEL, 4 * D_MODEL), lambda i: (0, 0)),
                  rows(256), rows(256), rows(256), rows(256),
                  pl.BlockSpec((4, 256, D_MODEL), lambda i: (0, 0, 0)),
                  pl.BlockSpec((D_MODEL, D_MODEL), lambda i: (0, 0))],
        out_specs=rows(D_MODEL),
        out_shape=jax.ShapeDtypeStruct((t, D_MODEL), F32),
        compiler_params=_row_params(),
        name="gate_mix",
    )(x2, g1, wg, *outs, wb, wo)


def _ffn_kernel(final, x_ref, g2_ref, wu_ref, wd_ref, gf_ref, y_ref):
    x = x_ref[...]
    hb = _rms(x, g2_ref[...]).astype(BF16)
    acc = jnp.zeros((TM, D_MODEL), F32)
    for jc in range(D_FF // D_MODEL):
        u = jnp.dot(hb, wu_ref[:, jc * D_MODEL:(jc + 1) * D_MODEL], preferred_element_type=F32)
        u = jnp.square(jnp.maximum(u, 0.0)).astype(BF16)
        acc = acc + jnp.dot(u, wd_ref[jc * D_MODEL:(jc + 1) * D_MODEL, :], preferred_element_type=F32)
    y = x + acc
    if final:
        y = _rms(y, gf_ref[...])
    y_ref[...] = y


def _ffn(x2, g2, wu, wd, gf, final):
    t = x2.shape[0]
    rows = pl.BlockSpec((TM, D_MODEL), lambda i: (i, 0))
    vec = pl.BlockSpec((1, D_MODEL), lambda i: (0, 0))
    return pl.pallas_call(
        functools.partial(_ffn_kernel, final),
        grid=(t // TM,),
        in_specs=[rows, vec, pl.BlockSpec((D_MODEL, D_FF), lambda i: (0, 0)),
                  pl.BlockSpec((D_FF, D_MODEL), lambda i: (0, 0)), vec],
        out_specs=rows,
        out_shape=jax.ShapeDtypeStruct((t, D_MODEL), F32),
        compiler_params=_row_params(),
        name="ffn",
    )(x2, g2, wu, wd, gf)


def _compress_weights(cmp_w, cmp_pe):
    wk = cmp_w[0].reshape(NSA_CMP_LEN, HEAD_DIM, HEAD_DIM)
    wv = cmp_w[1].reshape(NSA_CMP_LEN, HEAD_DIM, HEAD_DIM)
    zero = jnp.zeros_like(wk)
    full = jnp.concatenate([jnp.concatenate([wk, zero], axis=2), jnp.concatenate([zero, wv], axis=2)], axis=1)
    half = NSA_CMP_LEN // 2
    w_top = full[:half].reshape(half * 128, 128).astype(BF16)
    w_bot = full[half:].reshape(half * 128, 128).astype(BF16)
    pe = jnp.concatenate([cmp_pe[0], cmp_pe[1]], axis=1)
    pe2 = pe.reshape(2, half * 128)
    return w_top, w_bot, pe2


def _overlap_matrix(seq):
    n_rows = seq // NSA_CMP_STRIDE
    cs = np.arange(n_rows) * NSA_CMP_STRIDE
    sb = np.arange(LANES) * NSA_SEL_LEN
    ov = np.maximum(np.minimum(cs[:, None] + NSA_CMP_LEN, sb[None, :] + NSA_SEL_LEN)
                    - np.maximum(cs[:, None], sb[None, :]), 0).astype(np.float32) / NSA_CMP_LEN
    ov[:, seq // NSA_SEL_LEN:] = 0.0
    ov[(seq - NSA_CMP_LEN) // NSA_CMP_STRIDE + 1:, :] = 0.0
    return jnp.asarray(np.ascontiguousarray(ov.T))


def kernel(x, positions, norm1_g, w_in, mla_q_norm_g, mla_w_uq, mla_kv_norm_g, mla_w_ukv, nsa_cmp_pe, nsa_cmp_w,
           fox_f_bias, w_branch, w_out, norm2_g, w_up, w_down, final_g):
    batch, seq, _ = x.shape
    depth = w_in.shape[0]
    assert seq % TQ == 0 and seq % TM == 0 and seq // NSA_SEL_LEN <= LANES and TQ == TK
    assert seq // NSA_CMP_STRIDE <= LANES and (seq // NSA_CMP_STRIDE) % 8 == 0

    src, scale = _inproj_layout()
    q_src, k_src, v_src = _mla_up_layouts()
    cs3, place3 = _rope_tables(positions)
    head_mask = _head_masks(N_HEADS, 256)
    consts = _placement_constants(seq)
    eye = jnp.eye(128, dtype=F32)
    overlap_t = _overlap_matrix(seq)

    x2 = x.reshape(batch * seq, D_MODEL)
    for l in range(depth):
        w_packed = _pack_cols(w_in[l, :, :GATE_BASE], src, scale).astype(BF16)
        wg = w_in[l, :, GATE_BASE:].astype(BF16)
        wuq = _pack_cols(mla_w_uq[l], q_src, np.full(512, (MLA_NOPE + MLA_ROPE) ** -0.5 * LOG2E, np.float32)).astype(BF16)
        wuk = _pack_cols(mla_w_ukv[l], k_src, np.ones(512, np.float32)).astype(BF16)
        wuv = _pack_cols(mla_w_ukv[l], v_src, np.ones(256, np.float32)).astype(BF16)
        fb = jnp.zeros((1, 128), F32).at[0, :N_HEADS].set(fox_f_bias[l])
        g1 = norm1_g[l][None, :]

        a, bm, kvc, cidx, d, e = _inproj(x2, g1, w_packed, cs3, place3, mla_q_norm_g[l][None, :], wuq,
                                          mla_kv_norm_g[l][None, :], wuk, wuv, fb, seq)
        w_top, w_bot, pe2 = _compress_weights(nsa_cmp_w[l], nsa_cmp_pe[l])
        kvcmp = _compress(kvc.reshape(batch, seq // NSA_CMP_STRIDE, NSA_CMP_STRIDE * 128), pe2, w_top, w_bot)

        o_mla = _mla(a, consts, batch, seq)
        o_nsa = _nsa(bm, kvcmp, d, head_mask, overlap_t, consts, batch, seq)
        o_fox = _fox(e, d, head_mask, eye, consts, batch, seq)
        o_dsa = _dsa(bm, cidx, d, head_mask, consts, batch, seq)

        x2 = _mix(x2, g1, wg, (o_mla, o_nsa, o_fox, o_dsa), w_branch[l].astype(BF16), w_out[l].astype(BF16))
        x2 = _ffn(x2, norm2_g[l][None, :], w_up[l].astype(BF16), w_down[l].astype(BF16), final_g[None, :],
                  l == depth - 1)
    return x2.reshape(batch, seq, D_MODEL)
```

```python
import functools

import numpy as np
import jax
import jax.numpy as jnp
from jax import lax
from jax.experimental import pallas as pl
from jax.experimental.pallas import tpu as pltpu

F32 = jnp.float32
BF16 = jnp.bfloat16

D_MODEL = 1024
ROPE_THETA = 500000.0
NEG_INF = -1e30
NORM_EPS = 1e-6
HEAD_DIM = 64
N_HEADS = 4
D_FF = 4 * D_MODEL
MLA_NOPE, MLA_ROPE, MLA_V = 64, 32, 64
MLA_Q_LORA, MLA_KV_LORA = 256, 128
NSA_CMP_LEN, NSA_CMP_STRIDE, NSA_SEL_LEN = 32, 16, 64
NSA_TOP_N, NSA_WINDOW, NSA_FORCED_SCORE = 8, 512, 1e4
IDX_HEADS, IDX_DIM = 8, 32
DSA_TOP_K, DSA_KEEP_DIV = 256, 4
MLA_COLS, NSA_COLS, FOX_COLS, DSA_COLS = 416, 652, 772, 680
GATE_BASE = MLA_COLS + NSA_COLS + FOX_COLS + DSA_COLS

LANES = 128
TQ = 256
TK = 256
TM = 256
VMEM_LIMIT = 56 * 1024 * 1024
INT_MIN = -2 ** 31
LOG2E = 1.4426950408889634

P_CQ, P_CKV, P_KR = 0, 256, 384
P_NQ, P_DQ = 512, 768
P_KVC, P_KVS, P_KVW, P_KVD = 1024, 1152, 1280, 1408
P_QI, P_KI = 1536, 1792
P_G, P_W = 2048, 2176
P_FQ, P_FK, P_FV, P_FF = 2304, 2560, 2816, 3072
P_TOTAL = 3200


def _inproj_layout():
    src = -np.ones(P_TOTAL, np.int64)
    scale = np.ones(P_TOTAL, np.float32)
    mla, nsa, fox, dsa = 0, MLA_COLS, MLA_COLS + NSA_COLS, MLA_COLS + NSA_COLS + FOX_COLS
    ar = np.arange
    src[P_CQ:P_CQ + 256] = mla + ar(256)
    src[P_CKV:P_CKV + 128] = mla + 256 + ar(128)
    src[P_KR + 64:P_KR + 96] = mla + 384 + ar(32)
    src[P_NQ:P_NQ + 256] = nsa + ar(256)
    scale[P_NQ:P_NQ + 256] = HEAD_DIM ** -0.5 * LOG2E
    src[P_DQ:P_DQ + 256] = dsa + ar(256)
    scale[P_DQ:P_DQ + 256] = HEAD_DIM ** -0.5 * LOG2E
    for base, k_off, v_off, origin in ((P_KVC, 256, 320, nsa), (P_KVS, 384, 448, nsa),
                                       (P_KVW, 512, 576, nsa), (P_KVD, 256, 320, dsa)):
        src[base:base + 64] = origin + k_off + ar(64)
        src[base + 64:base + 128] = origin + v_off + ar(64)
    src[P_QI:P_QI + 256] = dsa + 384 + ar(256)
    scale[P_QI:P_QI + 256] = (IDX_DIM * IDX_HEADS) ** -0.5
    for r in range(IDX_HEADS):
        src[P_KI + 32 * r:P_KI + 32 * r + 32] = dsa + 640 + ar(32)
    src[P_G:P_G + 12] = nsa + 640 + ar(12)
    src[P_W:P_W + 8] = dsa + 672 + ar(8)
    src[P_FQ:P_FQ + 256] = fox + ar(256)
    scale[P_FQ:P_FQ + 256] = HEAD_DIM ** -0.5 * LOG2E
    src[P_FK:P_FK + 256] = fox + 256 + ar(256)
    src[P_FV:P_FV + 256] = fox + 512 + ar(256)
    src[P_FF:P_FF + 4] = fox + 768 + ar(4)
    return src, scale


def _pack_cols(w, src, scale):
    pieces, start = [], 0
    for end in range(1, len(src) + 1):
        run_ends = (end == len(src) or scale[end] != scale[start]
                    or (src[end] != src[end - 1] + 1 if src[start] >= 0 else src[end] >= 0))
        if run_ends:
            if src[start] >= 0:
                pieces.append(w[:, int(src[start]):int(src[start]) + end - start] * float(scale[start]))
            else:
                pieces.append(jnp.zeros((w.shape[0], end - start), w.dtype))
            start = end
    return jnp.concatenate(pieces, axis=1)


def _mla_up_layouts():
    q_src = -np.ones(512, np.int64)
    k_src = -np.ones(512, np.int64)
    v_src = np.zeros(256, np.int64)
    for h in range(N_HEADS):
        q_src[h * 128:h * 128 + 96] = h * 96 + np.arange(96)
        k_src[h * 128:h * 128 + 64] = h * 128 + np.arange(64)
        v_src[h * 64:h * 64 + 64] = h * 128 + 64 + np.arange(64)
    return q_src, k_src, v_src


def _rope_tables(positions):
    pos = positions.astype(F32).reshape(-1)[None, :]
    n = pos.shape[1]
    rots = (("mla", MLA_ROPE), ("head", HEAD_DIM // 4), ("idx", IDX_DIM // 4))
    inv = jnp.concatenate([ROPE_THETA ** (-jnp.arange(0, rot, 2, dtype=F32) / rot) for _, rot in rots])
    ang = inv[:, None] * pos
    cos_all, sin_all = jnp.cos(ang), jnp.sin(ang)
    parts, offs = [], {}
    col = row = 0
    for name, rot in rots:
        parts += [cos_all[row:row + rot // 2], sin_all[row:row + rot // 2]]
        offs[name] = (col, col + rot // 2, rot // 2)
        col += rot
        row += rot // 2
    one_col = col
    parts.append(jnp.ones((64 - col, n), F32))
    cs = jnp.concatenate(parts, axis=0)
    hi = cs.astype(BF16)
    rest = cs - hi.astype(F32)
    mid = rest.astype(BF16)
    low = (rest - mid.astype(F32)).astype(BF16)
    cs3 = jnp.concatenate([hi, mid, low, jnp.zeros((64, n), BF16)], axis=0).T

    place = np.zeros((64, 768), np.float32)

    def fill(cos_base, sin_base, name, group, first, n_groups):
        c0, s0, half = offs[name]
        place[one_col, cos_base:cos_base + 128] = 1.0
        for g in range(n_groups):
            lo = g * group + first
            for j in range(half):
                for lane, sign in ((lo + j, -1.0), (lo + half + j, 1.0)):
                    place[one_col, cos_base + lane] = 0.0
                    place[c0 + j, cos_base + lane] = 1.0
                    place[s0 + j, sin_base + lane] = sign

    fill(0, 128, "mla", 128, 64, 1)
    fill(256, 384, "head", 64, 0, 2)
    fill(512, 640, "idx", 32, 0, 4)
    place3 = np.concatenate([place, place, place, np.zeros((64, 768), np.float32)], axis=0)
    return cs3, jnp.asarray(place3, BF16)


def _head_masks(n_heads, width):
    lane = np.arange(width)
    return jnp.asarray((lane[None, :] // (width // n_heads) == np.arange(n_heads)[:, None]), BF16)


def _placement_constants(seq):
    d = np.arange(64)
    rk = np.zeros((128, 256), np.float32)
    pvs = np.zeros((128, 128), np.float32)
    pv4 = np.zeros((4, 256, 128), np.float32)
    pt4 = np.zeros((4, 128, 256), np.float32)
    pvs[64 + d, d] = 1.0
    for h in range(N_HEADS):
        rk[d, h * 64 + d] = 1.0
        pv4[h, h * 64 + d, d] = 1.0
        pt4[h, d, h * 64 + d] = 1.0
    one_row = np.zeros((1, 128), np.float32)
    one_row[0, 64] = 1.0
    r, c = np.arange(TQ)[:, None], np.arange(TK)[None, :]
    tril = np.where(c <= r, 0.0, NEG_INF).astype(np.float32)
    triu = np.where(c > r, 0.0, NEG_INF).astype(np.float32)
    void, free = np.full_like(tril, NEG_INF), np.zeros_like(tril)
    win_bias = np.stack([np.concatenate(t, axis=1) for t in ((tril, void, void), (free, tril, void),
                                                              (triu, free, tril))])
    nc = seq // TK
    key_blk = (np.arange(seq) // NSA_SEL_LEN).reshape(nc, 1, TK)
    expand = (key_blk == np.arange(LANES).reshape(1, LANES, 1)).astype(np.float32)
    bf = lambda a: jnp.asarray(a, BF16)
    return dict(rk=bf(rk), pvs=bf(pvs), pv4=bf(pv4), pt4=bf(pt4), one_row=jnp.asarray(one_row),
                tril=jnp.asarray(tril), win_bias=jnp.asarray(win_bias), expand=bf(expand * -NEG_INF),
                eye=bf(np.eye(256, dtype=np.float32)), big_eye=bf(np.eye(256, dtype=np.float32) * -NEG_INF),
                eyef=jnp.asarray(np.eye(128, dtype=np.float32)),
                trilt=jnp.asarray(np.ascontiguousarray(tril.T)))


def _rms(x, g):
    return x * lax.rsqrt(jnp.mean(x * x, axis=-1, keepdims=True) + NORM_EPS) * g


def _rope_tile(x, cos, sin, is_x1, half):
    fwd = pltpu.roll(x, LANES - half, axis=1)
    bwd = pltpu.roll(x, half, axis=1)
    return x * cos + jnp.where(is_x1, fwd, bwd) * sin


def _nt_dot(a, b):
    return lax.dot_general(a, b, (((1,), (1,)), ((), ())), preferred_element_type=F32)


def _attn_params():
    return pltpu.CompilerParams(dimension_semantics=("arbitrary", "arbitrary"), vmem_limit_bytes=VMEM_LIMIT)


def _row_params():
    return pltpu.CompilerParams(dimension_semantics=("arbitrary",), vmem_limit_bytes=VMEM_LIMIT)


def _inproj_kernel(tiles_per_seq, x_ref, g1_ref, w_ref, cs_ref, place_ref, gq_ref, wuq_ref, gkv_ref, wuk_ref,
                   wuv_ref, fb_ref, a_ref, b_ref, kvc_ref, c_ref, d_ref, e_ref, carry_ref):
    i = pl.program_id(0)
    x = x_ref[...]
    hb = _rms(x, g1_ref[...]).astype(BF16)
    z = jnp.dot(hb, w_ref[...], preferred_element_type=F32)

    lane = lax.broadcasted_iota(jnp.int32, (TM, LANES), 1)
    tab = jnp.dot(cs_ref[...], place_ref[...], preferred_element_type=F32)
    cos1, sin1 = tab[:, 0:128], tab[:, 128:256]
    cos3, sin3 = tab[:, 256:384], tab[:, 384:512]
    cos5, sin5 = tab[:, 512:640], tab[:, 640:768]
    x1_mla = (lane >= 64) & (lane < 80)
    x1_head = (lane & 63) < 8
    x1_idx = (lane & 31) < 4
    k_half = lane < 64
    cos4 = jnp.where(k_half, cos3, 1.0)
    sin4 = jnp.where(k_half, sin3, 0.0)

    cqn = _rms(z[:, P_CQ:P_CQ + 256], gq_ref[...]).astype(BF16)
    q_up = jnp.dot(cqn, wuq_ref[...], preferred_element_type=F32)
    ckvn = _rms(z[:, P_CKV:P_CKV + 128], gkv_ref[...]).astype(BF16)
    k_up = jnp.dot(ckvn, wuk_ref[...], preferred_element_type=F32)
    v_up = jnp.dot(ckvn, wuv_ref[...], preferred_element_type=F32)
    k_rot = _rope_tile(z[:, P_KR:P_KR + 128], cos1, sin1, x1_mla, 16)
    for h in range(N_HEADS):
        sl = slice(h * 128, (h + 1) * 128)
        a_ref[:, sl] = _rope_tile(q_up[:, sl], cos1, sin1, x1_mla, 16).astype(BF16)
        a_ref[:, 512 + h * 128:512 + (h + 1) * 128] = (k_up[:, sl] + k_rot).astype(BF16)
    a_ref[:, 1024:1280] = v_up.astype(BF16)

    for t in range(4):
        b_ref[:, t * 128:(t + 1) * 128] = _rope_tile(
            z[:, P_NQ + t * 128:P_NQ + (t + 1) * 128], cos3, sin3, x1_head, 8).astype(BF16)
    kvc_ref[...] = _rope_tile(z[:, P_KVC:P_KVC + 128], cos4, sin4, x1_head & k_half, 8).astype(BF16)
    for t, off in enumerate((P_KVS, P_KVW, P_KVD)):
        b_ref[:, 512 + t * 128:512 + (t + 1) * 128] = _rope_tile(
            z[:, off:off + 128], cos4, sin4, x1_head & k_half, 8).astype(BF16)

    for t in range(4):
        c_ref[:, t * 128:(t + 1) * 128] = _rope_tile(
            z[:, P_QI + t * 128:P_QI + (t + 1) * 128], cos5, sin5, x1_idx, 4).astype(BF16)

    d_ref[:, 0:128] = 1.0 / (1.0 + jnp.exp(-z[:, P_G:P_G + 128]))
    d_ref[:, 128:256] = z[:, P_W:P_W + 128]
    f = z[:, P_FF:P_FF + 128] + fb_ref[...]
    log_f = jnp.minimum(f, 0.0) - jnp.log(1.0 + jnp.exp(-jnp.abs(f)))

    @pl.when(i % tiles_per_seq == 0)
    def _():
        carry_ref[...] = jnp.zeros_like(carry_ref)

    row = lax.broadcasted_iota(jnp.int32, (TM, TM), 0)
    col = lax.broadcasted_iota(jnp.int32, (TM, TM), 1)
    tri = jnp.where(col <= row, 1.0, 0.0).astype(F32)
    c = jnp.dot(tri, log_f, preferred_element_type=F32, precision=lax.Precision.HIGHEST)
    c = c + carry_ref[0:1, :]
    d_ref[:, 256:384] = c
    carry_ref[...] = jnp.broadcast_to(c[TM - 1:TM, :], carry_ref.shape)

    e_ref[...] = z[:, P_FQ:P_FQ + 768].astype(BF16)


def _inproj(x2, g1, w_packed, cs3, place3, gq, wuq, gkv, wuk, wuv, fb, seq):
    t = x2.shape[0]
    full = lambda shape: pl.BlockSpec(shape, lambda i: (0, 0))
    rows = lambda w: pl.BlockSpec((TM, w), lambda i: (i, 0))
    return pl.pallas_call(
        functools.partial(_inproj_kernel, seq // TM),
        grid=(t // TM,),
        in_specs=[rows(D_MODEL), full((1, D_MODEL)), full((D_MODEL, P_TOTAL)), rows(256), full((256, 768)),
                  full((1, 256)), full((256, 512)), full((1, 128)), full((128, 512)), full((128, 256)),
                  full((1, 128))],
        out_specs=[rows(1280), rows(896), rows(128), rows(512), rows(384), rows(768)],
        out_shape=[jax.ShapeDtypeStruct((t, 1280), BF16), jax.ShapeDtypeStruct((t, 896), BF16),
                   jax.ShapeDtypeStruct((t, 128), BF16), jax.ShapeDtypeStruct((t, 512), BF16),
                   jax.ShapeDtypeStruct((t, 384), F32), jax.ShapeDtypeStruct((t, 768), BF16)],
        scratch_shapes=[pltpu.VMEM((8, 128), F32)],
        compiler_params=_row_params(),
        name="inproj",
    )(x2, g1, w_packed, cs3, place3, gq, wuq, gkv, wuk, wuv, fb)


def _compress_kernel(r_ref, pe_ref, wt_ref, wb_ref, o_ref):
    r = r_ref[0].astype(F32)
    top = jnp.dot((r + pe_ref[0:1, :]).astype(BF16), wt_ref[...], preferred_element_type=F32)
    bot = jnp.dot((r + pe_ref[1:2, :]).astype(BF16), wb_ref[...], preferred_element_type=F32)
    n = bot.shape[0]
    o_ref[0] = (top + pltpu.roll(bot, n - 1, axis=0)).astype(BF16)


def _compress(kvc3, pe2, w_top, w_bot):
    b, n, _ = kvc3.shape
    return pl.pallas_call(
        _compress_kernel,
        grid=(b,),
        in_specs=[pl.BlockSpec((1, n, 2048), lambda i: (i, 0, 0)), pl.BlockSpec((2, 2048), lambda i: (0, 0)),
                  pl.BlockSpec((2048, 128), lambda i: (0, 0)), pl.BlockSpec((2048, 128), lambda i: (0, 0))],
        out_specs=pl.BlockSpec((1, n, 128), lambda i: (i, 0, 0)),
        out_shape=jax.ShapeDtypeStruct((b, n, 128), BF16),
        compiler_params=_row_params(),
        name="nsa_compress",
    )(kvc3, pe2, w_top, w_bot)


def _flash_init(rows):
    return jnp.full((rows, 1), NEG_INF, F32), jnp.zeros((rows, LANES), F32)


def _flash_update(carry, s, v):
    m, acc = carry
    m_new = jnp.maximum(m, jnp.max(s, axis=-1, keepdims=True))
    p = jnp.exp2(s - m_new).astype(BF16)
    alpha = jnp.exp2(m - m_new)
    return m_new, alpha * acc + jnp.dot(p, v, preferred_element_type=F32)


def _flash_update_many(carries, scores, values):
    ms = [jnp.maximum(m, jnp.max(s, axis=-1, keepdims=True)) for (m, _), s in zip(carries, scores)]
    ps = [jnp.exp2(s - m).astype(BF16) for s, m in zip(scores, ms)]
    alphas = [jnp.exp2(m_old - m) for (m_old, _), m in zip(carries, ms)]
    pvs = [jnp.dot(p, v, preferred_element_type=F32) for p, v in zip(ps, values)]
    return tuple((m, a * acc + pv) for m, a, (_, acc), pv in zip(ms, alphas, carries, pvs))


def _key_group_loop(n, step, init):
    n4, r = n // 4, n % 4
    carry = lax.fori_loop(0, n4 - (r == 0).astype(jnp.int32), lambda g, cr: step(4 * g, 4, cr, False), init)
    tails = (lambda cr: step(n - 4, 4, cr, True),
             lambda cr: step(n - 1, 1, cr, True),
             lambda cr: step(n - 2, 2, cr, True),
             lambda cr: step(n - 1, 1, step(n - 3, 2, cr, False), True))
    return lax.switch(r, tails, carry)


def _add_diag(s, tril):
    if s.shape[1] == TK:
        return s + tril
    return jnp.concatenate([s[:, :s.shape[1] - TK], s[:, s.shape[1] - TK:] + tril], axis=1)


def _flash_out(acc):
    return acc / jnp.maximum(acc[:, 64:65], 1e-30)


def _place_heads(outs, pt_ref):
    res = None
    for h, o in enumerate(outs):
        t = jnp.dot(o.astype(BF16), pt_ref[h], preferred_element_type=F32)
        res = t if res is None else res + t
    return res.astype(BF16)


def _chunk_rows(c, width=1):
    return pl.ds(pl.multiple_of(c * TK, TK), width * TK)


def _stack_heads(x, mask_ref, n):
    return jnp.concatenate([x * mask_ref[h:h + 1, :] for h in range(n)], axis=0)


def _mla_kernel(q_ref, k_ref, v_ref, pv_ref, one_ref, pt_ref, tril_ref, o_ref, vaug_ref):
    i = pl.program_id(1)

    @pl.when(i == 0)
    def _():
        v = v_ref[...]
        for h in range(N_HEADS):
            vaug_ref[h] = (jnp.dot(v, pv_ref[h], preferred_element_type=F32) + one_ref[...]).astype(BF16)

    qs = [q_ref[:, h * 128:(h + 1) * 128] for h in range(N_HEADS)]

    def step(c, width, carries, diag):
        rows = _chunk_rows(c, width)
        scores = [_nt_dot(qs[h], k_ref[rows, h * 128:(h + 1) * 128]) for h in range(N_HEADS)]
        if diag:
            scores = [_add_diag(s, tril_ref[...]) for s in scores]
        return _flash_update_many(carries, scores, [vaug_ref[h, rows, :] for h in range(N_HEADS)])

    carries = _key_group_loop(i + 1, step, tuple(_flash_init(TQ) for _ in range(N_HEADS)))
    o_ref[...] = _place_heads([_flash_out(acc) for _, acc in carries], pt_ref)


def _mla(a, k, batch, seq):
    nq = seq // TQ
    const2 = lambda shape: pl.BlockSpec(shape, lambda b, i: (0, 0))
    const3 = lambda shape: pl.BlockSpec(shape, lambda b, i: (0, 0, 0))
    return pl.pallas_call(
        _mla_kernel,
        grid=(batch, nq),
        in_specs=[pl.BlockSpec((TQ, 512), lambda b, i: (b * nq + i, 0)),
                  pl.BlockSpec((seq, 512), lambda b, i: (b, 1)),
                  pl.BlockSpec((seq, 256), lambda b, i: (b, 4)),
                  const3((4, 256, 128)), const2((1, 128)), const3((4, 128, 256)), const2((TQ, TK))],
        out_specs=pl.BlockSpec((TQ, 256), lambda b, i: (b * nq + i, 0)),
        out_shape=jax.ShapeDtypeStruct((batch * seq, 256), BF16),
        scratch_shapes=[pltpu.VMEM((N_HEADS, seq, 128), BF16)],
        compiler_params=_attn_params(),
        name="mla_attn",
    )(a, a, a, k["pv4"], k["one_row"], k["pt4"], k["tril"])


def _fox_kernel(n_chunks, q_ref, k_ref, v_ref, call_ref, hm_ref, eye_ref, pv_ref, one_ref, pt_ref,
                tril_ref, o_ref, vaug_ref, ct_ref):
    i = pl.program_id(1)

    @pl.when(i == 0)
    def _():
        v = v_ref[...]
        for h in range(N_HEADS):
            vaug_ref[h] = (jnp.dot(v, pv_ref[h], preferred_element_type=F32) + one_ref[...]).astype(BF16)
        ct = lax.dot_general(eye_ref[...], call_ref[...], (((1,), (1,)), ((), ())),
                             preferred_element_type=F32, precision=lax.Precision.HIGHEST)
        for c in range(n_chunks):
            ct_ref[c] = ct[0:8, c * TK:(c + 1) * TK] * LOG2E

    q_all = q_ref[...]
    qs = [q_all * hm_ref[h:h + 1, :] for h in range(N_HEADS)]

    def step(c, width, carries, diag):
        rows = _chunk_rows(c, width)
        k = k_ref[rows, :]
        cs = jnp.concatenate([ct_ref[c + u] for u in range(width)], axis=1)
        scores = [_nt_dot(qs[h], k) - cs[h:h + 1, :] for h in range(N_HEADS)]
        if diag:
            scores = [_add_diag(s, tril_ref[...]) for s in scores]
        return _flash_update_many(carries, scores, [vaug_ref[h, rows, :] for h in range(N_HEADS)])

    carries = _key_group_loop(i + 1, step, tuple(_flash_init(TQ) for _ in range(N_HEADS)))
    o_ref[...] = _place_heads([_flash_out(acc) for _, acc in carries], pt_ref)


def _fox(e, d, head_mask, eye, k, batch, seq):
    nq = seq // TQ
    nc = seq // TK
    const2 = lambda shape: pl.BlockSpec(shape, lambda b, i: (0, 0))
    const3 = lambda shape: pl.BlockSpec(shape, lambda b, i: (0, 0, 0))
    return pl.pallas_call(
        functools.partial(_fox_kernel, nc),
        grid=(batch, nq),
        in_specs=[pl.BlockSpec((TQ, 256), lambda b, i: (b * nq + i, 0)),
                  pl.BlockSpec((seq, 256), lambda b, i: (b, 1)),
                  pl.BlockSpec((seq, 256), lambda b, i: (b, 2)),
                  pl.BlockSpec((seq, 128), lambda b, i: (b, 2)),
                  const2((N_HEADS, 256)), const2((128, 128)),
                  const3((4, 256, 128)), const2((1, 128)), const3((4, 128, 256)), const2((TQ, TK))],
        out_specs=pl.BlockSpec((TQ, 256), lambda b, i: (b * nq + i, 0)),
        out_shape=jax.ShapeDtypeStruct((batch * seq, 256), BF16),
        scratch_shapes=[pltpu.VMEM((N_HEADS, seq, 128), BF16), pltpu.VMEM((nc, 8, TK), F32)],
        compiler_params=_attn_params(),
        name="fox_attn",
    )(e, e, e, d, head_mask, eye, k["pv4"], k["one_row"], k["pt4"], k["tril"])


def _nsa_kernel(n_sel_blocks, q_ref, kvs_ref, kvw_ref, kvc_ref, g_ref, hm_ref, rk_ref, pvs_ref, one_ref,
                pt_ref, tril_ref, winb_ref, ex_ref, ovt_ref, eye_ref, o_ref, ks_ref, vs_ref, kw_ref, vw_ref, kc_ref,
                vc_ref):
    i = pl.program_id(1)

    @pl.when(i == 0)
    def _():
        rk, pvs, one = rk_ref[...], pvs_ref[...], one_ref[...]
        for src_ref, k_dst, v_dst in ((kvs_ref, ks_ref, vs_ref), (kvw_ref, kw_ref, vw_ref)):
            kv = src_ref[...]
            k_dst[...] = jnp.dot(kv, rk, preferred_element_type=F32).astype(BF16)
            v_dst[...] = (jnp.dot(kv, pvs, preferred_element_type=F32) + one).astype(BF16)
        kv = kvc_ref[0]
        kc_ref[...] = jnp.dot(kv, rk, preferred_element_type=F32).astype(BF16)
        vc_ref[...] = jnp.dot(kv, pvs, preferred_element_type=F32).astype(BF16)

    g = g_ref[...]
    q_stack = _stack_heads(q_ref[...], hm_ref, N_HEADS)
    n_cmp_rows = kc_ref.shape[0]
    head = lambda x, h: x[h * TQ:(h + 1) * TQ]

    t_c = i * TQ + lax.broadcasted_iota(jnp.int32, (TQ, n_cmp_rows), 0)
    c_idx = lax.broadcasted_iota(jnp.int32, (TQ, n_cmp_rows), 1)
    cmp_mask = c_idx * NSA_CMP_STRIDE + (NSA_CMP_LEN - 1) <= t_c
    s_cmp = _nt_dot(q_stack, kc_ref[...])
    p_sum = jnp.zeros((TQ, n_cmp_rows), F32)
    o_cmp = []
    for h in range(N_HEADS):
        s = jnp.where(cmp_mask, head(s_cmp, h), NEG_INF)
        m = jnp.max(s, axis=-1, keepdims=True)
        p = jnp.where(cmp_mask, jnp.exp2(s - m), 0.0)
        p = p / jnp.maximum(jnp.sum(p, axis=-1, keepdims=True), 1e-30)
        p_sum = p_sum + p
        o_cmp.append(jnp.dot(p.astype(BF16), vc_ref[...], preferred_element_type=F32))

    imp = lax.dot_general(ovt_ref[...], p_sum, (((1,), (1,)), ((), ())), preferred_element_type=F32,
                          precision=lax.Precision.HIGHEST)[0:n_sel_blocks]
    t_r = i * TQ + lax.broadcasted_iota(jnp.int32, (n_sel_blocks, TQ), 1)
    j = lax.broadcasted_iota(jnp.int32, (n_sel_blocks, TQ), 0)
    t_blk = t_r >> 6
    forced = (j == 0) | (j == t_blk) | (j == t_blk - 1)
    imp = jnp.where(forced, NSA_FORCED_SCORE, imp)
    imp = jnp.where(j * NSA_SEL_LEN <= t_r, imp, NEG_INF)
    rank = jnp.zeros((n_sel_blocks, TQ), F32)
    for i2 in range(n_sel_blocks):
        row = imp[i2:i2 + 1, :]
        beats = jnp.where(row > imp, 1.0, jnp.where(row == imp, jnp.where(j > i2, 1.0, 0.0), 0.0))
        rank = rank + beats
    dropped_t = jnp.concatenate([jnp.where(rank < float(NSA_TOP_N), 0.0, -1.0),
                                 jnp.full((LANES - n_sel_blocks, TQ), -1.0, F32)], axis=0).astype(BF16)
    dropped = _nt_dot(eye_ref[...], dropped_t).astype(BF16)

    def sel_step(c, width, carry, diag):
        rows = _chunk_rows(c, width)
        expand = jnp.concatenate([ex_ref[c + u] for u in range(width)], axis=1)
        pen = jnp.dot(dropped, expand, preferred_element_type=F32)
        if diag:
            pen = _add_diag(pen, tril_ref[...])
        s_all = _nt_dot(q_stack, ks_ref[rows, :])
        s = jnp.concatenate([head(s_all, h) + pen for h in range(N_HEADS)], axis=0)
        return _flash_update(carry, s, vs_ref[rows, :])

    _, sel_acc = _key_group_loop(i + 1, sel_step, _flash_init(N_HEADS * TQ))

    rows = _chunk_rows(jnp.maximum(i - 2, 0), 3)
    pen = winb_ref[jnp.minimum(i, 2)]
    s_all = _nt_dot(q_stack, kw_ref[rows, :])
    s = jnp.concatenate([head(s_all, h) + pen for h in range(N_HEADS)], axis=0)
    _, win_acc = _flash_update(_flash_init(N_HEADS * TQ), s, vw_ref[rows, :])

    o_sel = _flash_out(sel_acc)
    o_win = _flash_out(win_acc)
    outs = []
    for h in range(N_HEADS):
        outs.append(g[:, h:h + 1] * o_cmp[h] + g[:, 4 + h:5 + h] * head(o_sel, h)
                    + g[:, 8 + h:9 + h] * head(o_win, h))
    o_ref[...] = _place_heads(outs, pt_ref)


def _nsa(bm, kvcmp, d, head_mask, overlap_t, k, batch, seq):
    nq = seq // TQ
    nc = seq // TK
    n_cmp_rows = kvcmp.shape[1]
    const2 = lambda shape: pl.BlockSpec(shape, lambda b, i: (0, 0))
    const3 = lambda shape: pl.BlockSpec(shape, lambda b, i: (0, 0, 0))
    return pl.pallas_call(
        functools.partial(_nsa_kernel, seq // NSA_SEL_LEN),
        grid=(batch, nq),
        in_specs=[pl.BlockSpec((TQ, 256), lambda b, i: (b * nq + i, 0)),
                  pl.BlockSpec((seq, 128), lambda b, i: (b, 4)),
                  pl.BlockSpec((seq, 128), lambda b, i: (b, 5)),
                  pl.BlockSpec((1, n_cmp_rows, 128), lambda b, i: (b, 0, 0)),
                  pl.BlockSpec((TQ, 128), lambda b, i: (b * nq + i, 0)),
                  const2((N_HEADS, 256)), const2((128, 256)), const2((128, 128)), const2((1, 128)),
                  const3((4, 128, 256)), const2((TQ, TK)), const3((3, TQ, 3 * TK)), const3((nc, LANES, TK)),
                  const2((LANES, n_cmp_rows)), const2((256, 256))],
        out_specs=pl.BlockSpec((TQ, 256), lambda b, i: (b * nq + i, 0)),
        out_shape=jax.ShapeDtypeStruct((batch * seq, 256), BF16),
        scratch_shapes=[pltpu.VMEM((seq, 256), BF16), pltpu.VMEM((seq, 128), BF16),
                        pltpu.VMEM((seq, 256), BF16), pltpu.VMEM((seq, 128), BF16),
                        pltpu.VMEM((n_cmp_rows, 256), BF16), pltpu.VMEM((n_cmp_rows, 128), BF16)],
        compiler_params=_attn_params(),
        name="nsa_attn",
    )(bm, bm, bm, kvcmp, d, head_mask, k["rk"], k["pvs"], k["one_row"], k["pt4"], k["tril"], k["win_bias"],
      k["expand"], overlap_t, k["eye"])


def _dsa_kernel(n_keep, idx_bits, q_ref, kvd_ref, qi_ref, ki_ref, w_ref, hm_ref, rk_ref, pvs_ref,
                one_ref, pt_ref, tril_ref, trilt_ref, eye_ref, eyef_ref, bigeye_ref, o_ref, kd_ref, vd_ref, sc_ref,
                qit_ref):
    i = pl.program_id(1)

    @pl.when(i == 0)
    def _():
        kv = kvd_ref[...]
        kd_ref[...] = jnp.dot(kv, rk_ref[...], preferred_element_type=F32).astype(BF16)
        vd_ref[...] = (jnp.dot(kv, pvs_ref[...], preferred_element_type=F32) + one_ref[...]).astype(BF16)

    head = lambda x, h: x[h * TQ:(h + 1) * TQ]

    qi_t = _nt_dot(eye_ref[...], qi_ref[...])
    feat = lax.broadcasted_iota(jnp.int32, qi_t.shape, 0)
    for h in range(IDX_HEADS):
        qit_ref[:, h * TQ:(h + 1) * TQ] = jnp.where((feat >> 5) == h, qi_t, 0.0).astype(BF16)
    w_t = lax.dot_general(eyef_ref[...], w_ref[...], (((1,), (1,)), ((), ())),
                          preferred_element_type=F32, precision=lax.Precision.HIGHEST)
    ws = [w_t[h:h + 1, :] for h in range(IDX_HEADS)]

    def score_chunk(c, bias):
        logits = jnp.dot(ki_ref[_chunk_rows(c), :], qit_ref[...], preferred_element_type=F32)
        score = jnp.maximum(logits[:, 0:TQ], 0.0) * ws[0]
        for h in range(1, IDX_HEADS):
            score = score + jnp.maximum(logits[:, h * TQ:(h + 1) * TQ], 0.0) * ws[h]
        if bias is not None:
            score = score + bias
        sc_ref[c] = score

    def score_body(c, carry):
        score_chunk(c, None)
        return carry

    lax.fori_loop(0, i, score_body, 0)
    score_chunk(i, trilt_ref[...])

    def count(pred):
        def body(c, part):
            hit = jnp.where(pred(sc_ref[c], c), 1.0, 0.0)
            return part + jnp.sum(hit.reshape(TK // 8, 8, TQ), axis=0)
        part = lax.fori_loop(0, i + 1, body, jnp.zeros((8, TQ), F32))
        return jnp.sum(part, axis=0, keepdims=True)

    def key_to_float(key):
        bits = jnp.where(key < 0, key ^ jnp.int32(0x7FFFFFFF), key)
        return lax.bitcast_convert_type(bits, F32)

    keep = float(n_keep)
    total = ((i + 1) * TK).astype(F32)
    cnt0 = count(lambda s, c: s >= 0.0)
    thr0 = jnp.where(cnt0 >= keep, jnp.int32(0), jnp.int32(INT_MIN))
    cnt0 = jnp.where(cnt0 >= keep, cnt0, total)

    def thr_bit(it, carry):
        thr, cnt_thr = carry
        cand = thr | jnp.left_shift(jnp.int32(1), 30 - it)
        cand_f = key_to_float(cand)
        cnt = count(lambda s, c: s >= cand_f)
        take = cnt >= keep
        return jnp.where(take, cand, thr), jnp.where(take, cnt, cnt_thr)

    thr, cnt_thr = lax.fori_loop(0, 31, thr_bit, (thr0, cnt0))
    thr_f = key_to_float(thr)

    def key_index(c):
        return c * TK + lax.broadcasted_iota(jnp.int32, (TK, TQ), 0)

    def resolve_ties(_):
        need = keep - count(lambda s, c: s > thr_f)

        def cut_bit(it, cut):
            cand = cut | jnp.left_shift(jnp.int32(1), idx_bits - 1 - it)
            cnt = count(lambda s, c: (s == thr_f) & (key_index(c) < cand))
            return jnp.where(cnt < need, cand, cut)

        return lax.fori_loop(0, idx_bits, cut_bit, jnp.zeros((1, TQ), jnp.int32))

    has_excess = jnp.max(jnp.abs(cnt_thr - keep)) > 0.0
    cut = lax.cond(has_excess, resolve_ties, lambda _: jnp.full((1, TQ), 2 ** idx_bits, jnp.int32), 0)

    q_stack = _stack_heads(q_ref[...], hm_ref, N_HEADS)
    big_eye = bigeye_ref[...]

    def attn_step(c, width, carry, diag):
        rows = _chunk_rows(c, width)
        sc = jnp.concatenate([sc_ref[c + u] for u in range(width)], axis=0)
        k_idx = c * TK + lax.broadcasted_iota(jnp.int32, sc.shape, 0)
        tie = jnp.where(k_idx <= cut, 0.0, -1.0)
        dropped_t = jnp.where(sc > thr_f, 0.0, jnp.where(sc == thr_f, tie, -1.0)).astype(BF16)
        pen = _nt_dot(big_eye, dropped_t)
        if diag:
            pen = _add_diag(pen, tril_ref[...])
        s_all = _nt_dot(q_stack, kd_ref[rows, :])
        s = jnp.concatenate([head(s_all, h) + pen for h in range(N_HEADS)], axis=0)
        return _flash_update(carry, s, vd_ref[rows, :])

    _, acc = _key_group_loop(i + 1, attn_step, _flash_init(N_HEADS * TQ))
    out = _flash_out(acc)
    o_ref[...] = _place_heads([head(out, h) for h in range(N_HEADS)], pt_ref)


def _dsa(bm, cidx, d, head_mask, k, batch, seq):
    nq = seq // TQ
    nc = seq // TK
    n_keep = min(DSA_TOP_K, seq // DSA_KEEP_DIV)
    assert n_keep <= TK
    idx_bits = int(np.log2(seq))
    assert 2 ** idx_bits == seq
    const2 = lambda shape: pl.BlockSpec(shape, lambda b, i: (0, 0))
    const3 = lambda shape: pl.BlockSpec(shape, lambda b, i: (0, 0, 0))
    return pl.pallas_call(
        functools.partial(_dsa_kernel, n_keep, idx_bits),
        grid=(batch, nq),
        in_specs=[pl.BlockSpec((TQ, 256), lambda b, i: (b * nq + i, 1)),
                  pl.BlockSpec((seq, 128), lambda b, i: (b, 6)),
                  pl.BlockSpec((TQ, 256), lambda b, i: (b * nq + i, 0)),
                  pl.BlockSpec((seq, 256), lambda b, i: (b, 1)),
                  pl.BlockSpec((TQ, 128), lambda b, i: (b * nq + i, 1)),
                  const2((N_HEADS, 256)), const2((128, 256)), const2((128, 128)),
                  const2((1, 128)), const3((4, 128, 256)), const2((TQ, TK)), const2((TK, TQ)),
                  const2((256, 256)), const2((128, 128)), const2((256, 256))],
        out_specs=pl.BlockSpec((TQ, 256), lambda b, i: (b * nq + i, 0)),
        out_shape=jax.ShapeDtypeStruct((batch * seq, 256), BF16),
        scratch_shapes=[pltpu.VMEM((seq, 256), BF16), pltpu.VMEM((seq, 128), BF16),
                        pltpu.VMEM((nc, TK, TQ), F32), pltpu.VMEM((256, IDX_HEADS * TQ), BF16)],
        compiler_params=_attn_params(),
        name="dsa_attn",
    )(bm, bm, cidx, cidx, d, head_mask, k["rk"], k["pvs"], k["one_row"], k["pt4"], k["tril"], k["trilt"],
      k["eye"], k["eyef"], k["big_eye"])


def _mix_kernel(x_ref, g1_ref, wg_ref, o0_ref, o1_ref, o2_ref, o3_ref, wb_ref, wo_ref, y_ref):
    x = x_ref[...]
    hb = _rms(x, g1_ref[...]).astype(BF16)
    mixed = jnp.zeros((TM, D_MODEL), F32)
    for n, o_ref in enumerate((o0_ref, o1_ref, o2_ref, o3_ref)):
        zg = jnp.dot(hb, wg_ref[:, n * D_MODEL:(n + 1) * D_MODEL], preferred_element_type=F32)
        gate = 1.0 / (1.0 + jnp.exp(-zg))
        lifted = jnp.dot(o_ref[...], wb_ref[n], preferred_element_type=F32)
        mixed = mixed + gate * lifted
    y_ref[...] = x + jnp.dot(mixed.astype(BF16), wo_ref[...], preferred_element_type=F32)


def _mix(x2, g1, wg, outs, wb, wo):
    t = x2.shape[0]
    rows = lambda w: pl.BlockSpec((TM, w), lambda i: (i, 0))
    return pl.pallas_call(
        _mix_kernel,
        grid=(t // TM,),
        in_specs=[rows(D_MODEL), pl.BlockSpec((1, D_MODEL), lambda i: (0, 0)),
                  pl.BlockSpec((D_MODEL, 4 * D_MODEL), lambda i: (0, 0)),
                  rows(256), rows(256), rows(256), rows(256),
                  pl.BlockSpec((4, 256, D_MODEL), lambda i: (0, 0, 0)),
                  pl.BlockSpec((D_MODEL, D_MODEL), lambda i: (0, 0))],
        out_specs=rows(D_MODEL),
        out_shape=jax.ShapeDtypeStruct((t, D_MODEL), F32),
        compiler_params=_row_params(),
        name="gate_mix",
    )(x2, g1, wg, *outs, wb, wo)


def _ffn_kernel(final, x_ref, g2_ref, wu_ref, wd_ref, gf_ref, y_ref):
    x = x_ref[...]
    hb = _rms(x, g2_ref[...]).astype(BF16)
    acc = jnp.zeros((TM, D_MODEL), F32)
    for jc in range(D_FF // D_MODEL):
        u = jnp.dot(hb, wu_ref[:, jc * D_MODEL:(jc + 1) * D_MODEL], preferred_element_type=F32)
        u = jnp.square(jnp.maximum(u, 0.0)).astype(BF16)
        acc = acc + jnp.dot(u, wd_ref[jc * D_MODEL:(jc + 1) * D_MODEL, :], preferred_element_type=F32)
    y = x + acc
    if final:
        y = _rms(y, gf_ref[...])
    y_ref[...] = y


def _ffn(x2, g2, wu, wd, gf, final):
    t = x2.shape[0]
    rows = pl.BlockSpec((TM, D_MODEL), lambda i: (i, 0))
    vec = pl.BlockSpec((1, D_MODEL), lambda i: (0, 0))
    return pl.pallas_call(
        functools.partial(_ffn_kernel, final),
        grid=(t // TM,),
        in_specs=[rows, vec, pl.BlockSpec((D_MODEL, D_FF), lambda i: (0, 0)),
                  pl.BlockSpec((D_FF, D_MODEL), lambda i: (0, 0)), vec],
        out_specs=rows,
        out_shape=jax.ShapeDtypeStruct((t, D_MODEL), F32),
        compiler_params=_row_params(),
        name="ffn",
    )(x2, g2, wu, wd, gf)


def _compress_weights(cmp_w, cmp_pe):
    wk = cmp_w[0].reshape(NSA_CMP_LEN, HEAD_DIM, HEAD_DIM)
    wv = cmp_w[1].reshape(NSA_CMP_LEN, HEAD_DIM, HEAD_DIM)
    zero = jnp.zeros_like(wk)
    full = jnp.concatenate([jnp.concatenate([wk, zero], axis=2), jnp.concatenate([zero, wv], axis=2)], axis=1)
    half = NSA_CMP_LEN // 2
    w_top = full[:half].reshape(half * 128, 128).astype(BF16)
    w_bot = full[half:].reshape(half * 128, 128).astype(BF16)
    pe = jnp.concatenate([cmp_pe[0], cmp_pe[1]], axis=1)
    pe2 = pe.reshape(2, half * 128)
    return w_top, w_bot, pe2


def _overlap_matrix(seq):
    n_rows = seq // NSA_CMP_STRIDE
    cs = np.arange(n_rows) * NSA_CMP_STRIDE
    sb = np.arange(LANES) * NSA_SEL_LEN
    ov = np.maximum(np.minimum(cs[:, None] + NSA_CMP_LEN, sb[None, :] + NSA_SEL_LEN)
                    - np.maximum(cs[:, None], sb[None, :]), 0).astype(np.float32) / NSA_CMP_LEN
    ov[:, seq // NSA_SEL_LEN:] = 0.0
    ov[(seq - NSA_CMP_LEN) // NSA_CMP_STRIDE + 1:, :] = 0.0
    return jnp.asarray(np.ascontiguousarray(ov.T))


def kernel(x, positions, norm1_g, w_in, mla_q_norm_g, mla_w_uq, mla_kv_norm_g, mla_w_ukv, nsa_cmp_pe, nsa_cmp_w,
           fox_f_bias, w_branch, w_out, norm2_g, w_up, w_down, final_g):
    batch, seq, _ = x.shape
    depth = w_in.shape[0]
    assert seq % TQ == 0 and seq % TM == 0 and seq // NSA_SEL_LEN <= LANES and TQ == TK
    assert seq // NSA_CMP_STRIDE <= LANES and (seq // NSA_CMP_STRIDE) % 8 == 0

    src, scale = _inproj_layout()
    q_src, k_src, v_src = _mla_up_layouts()
    cs3, place3 = _rope_tables(positions)
    head_mask = _head_masks(N_HEADS, 256)
    consts = _placement_constants(seq)
    eye = jnp.eye(128, dtype=F32)
    overlap_t = _overlap_matrix(seq)

    x2 = x.reshape(batch * seq, D_MODEL)
    for l in range(depth):
        w_packed = _pack_cols(w_in[l, :, :GATE_BASE], src, scale).astype(BF16)
        wg = w_in[l, :, GATE_BASE:].astype(BF16)
        wuq = _pack_cols(mla_w_uq[l], q_src, np.full(512, (MLA_NOPE + MLA_ROPE) ** -0.5 * LOG2E, np.float32)).astype(BF16)
        wuk = _pack_cols(mla_w_ukv[l], k_src, np.ones(512, np.float32)).astype(BF16)
        wuv = _pack_cols(mla_w_ukv[l], v_src, np.ones(256, np.float32)).astype(BF16)
        fb = jnp.zeros((1, 128), F32).at[0, :N_HEADS].set(fox_f_bias[l])
        g1 = norm1_g[l][None, :]

        a, bm, kvc, cidx, d, e = _inproj(x2, g1, w_packed, cs3, place3, mla_q_norm_g[l][None, :], wuq,
                                          mla_kv_norm_g[l][None, :], wuk, wuv, fb, seq)
        w_top, w_bot, pe2 = _compress_weights(nsa_cmp_w[l], nsa_cmp_pe[l])
        kvcmp = _compress(kvc.reshape(batch, seq // NSA_CMP_STRIDE, NSA_CMP_STRIDE * 128), pe2, w_top, w_bot)

        o_mla = _mla(a, consts, batch, seq)
        o_nsa = _nsa(bm, kvcmp, d, head_mask, overlap_t, consts, batch, seq)
        o_fox = _fox(e, d, head_mask, eye, consts, batch, seq)
        o_dsa = _dsa(bm, cidx, d, head_mask, consts, batch, seq)

        x2 = _mix(x2, g1, wg, (o_mla, o_nsa, o_fox, o_dsa), w_branch[l].astype(BF16), w_out[l].astype(BF16))
        x2 = _ffn(x2, norm2_g[l][None, :], w_up[l].astype(BF16), w_down[l].astype(BF16), final_g[None, :],
                  l == depth - 1)
    return x2.reshape(batch, seq, D_MODEL)
```
